```python
import jax, jax.numpy as jnp
from jax import lax
import numpy as np

D_MODEL = 2048
BATCH = 16
SEQ = 2048
DEPTH = 4

GRID_W = 64
CTX_LEN = 256
N_MIXERS = 3
CONV_WIDTH = 31
FNET_GROUPS = 8
FNET_GROUP_W = D_MODEL // FNET_GROUPS
NA_HEADS = 16
NA_HEAD_DIM = D_MODEL // NA_HEADS
WIN_H = 8
WIN_W = 16
D_FF = 5632
FFN_CONV_WIDTH = 3
EPS = 1e-6
N_A = (DEPTH + 2) // 3
N_B = (DEPTH + 1) // 3
N_C = DEPTH // 3

kernel_name = "hybrid_conformer_fnet_natten_dit"


def rms_norm(t, g):
    tf = t.astype(jnp.float32)
    y = tf * lax.rsqrt(jnp.mean(tf * tf, axis=-1, keepdims=True) + EPS)
    return (y * g.astype(jnp.float32)).astype(t.dtype)


def layer_norm(t, g, b):
    tf = t.astype(jnp.float32)
    mu = jnp.mean(tf, axis=-1, keepdims=True)
    var = jnp.mean(jnp.square(tf - mu), axis=-1, keepdims=True)
    y = (tf - mu) * lax.rsqrt(var + EPS)
    return (y * g.astype(jnp.float32) + b.astype(jnp.float32)).astype(t.dtype)


def modulate(t, shift, scale):
    return t * (1.0 + scale) + shift


def depthwise_conv(t, w, b):
    k = w.shape[0]
    y = lax.conv_general_dilated(t, w[:, None, :], window_strides=(1,), padding=[(k // 2, k // 2)],
                                 dimension_numbers=("NWC", "WIO", "NWC"), feature_group_count=t.shape[-1])
    return y + b


def conformer_conv(h, w_pw1, b_pw1, w_dw, b_dw, ln_g, ln_b, w_pw2, b_pw2):
    u = h @ w_pw1 + b_pw1
    a, g = jnp.split(u, 2, axis=-1)
    u = a * jax.nn.sigmoid(g)
    u = depthwise_conv(u, w_dw, b_dw)
    u = jax.nn.silu(layer_norm(u, ln_g, ln_b))
    return u @ w_pw2 + b_pw2


def fourier_mix(h, w_out, b_out):
    bsz, length, d = h.shape
    hg = h.astype(jnp.float32).reshape(bsz, length, FNET_GROUPS, FNET_GROUP_W)
    f = jnp.fft.fftn(hg, axes=(1, 3), norm="ortho").real
    return f.reshape(bsz, length, d).astype(h.dtype) @ w_out + b_out


def conv_ffn(h, w_up, w_dw, b_dw, w_down):
    u = depthwise_conv(h @ w_up, w_dw, b_dw)
    v, g = jnp.split(u, 2, axis=-1)
    return (jax.nn.silu(g) * v) @ w_down


def neighbourhood_attention(h, hc, w_qkv, q_g, k_g, rpb, w_o, ctx_out):
    bsz, length, d = h.shape
    rows = length // GRID_W
    kh = min(WIN_H, rows)
    n_loc = kh * GRID_W

    def proj(t):
        qkv = (t @ w_qkv).reshape(t.shape[0], t.shape[1], 3, NA_HEADS, NA_HEAD_DIM)
        q = rms_norm(qkv[:, :, 0], q_g) * (NA_HEAD_DIM ** -0.5)
        k = rms_norm(qkv[:, :, 1], k_g)
        return q, k, qkv[:, :, 2]

    q, k, v = proj(h)
    qc, kc, vc = proj(hc)

    q_rows = q.reshape(bsz, rows, GRID_W, NA_HEADS, NA_HEAD_DIM).transpose(1, 0, 2, 3, 4)
    k_grid = k.reshape(bsz, rows, GRID_W, NA_HEADS, NA_HEAD_DIM)
    v_grid = v.reshape(bsz, rows, GRID_W, NA_HEADS, NA_HEAD_DIM)

    row_start = jnp.asarray(np.clip(np.arange(rows) - kh // 2, 0, rows - kh), jnp.int32)
    r_idx = jnp.arange(rows, dtype=jnp.int32)
    qcol = np.arange(GRID_W)[:, None]
    kcol = np.arange(GRID_W)[None, :]
    cstart = np.clip(qcol - WIN_W // 2, 0, GRID_W - WIN_W)
    col_mask = (kcol >= cstart) & (kcol < cstart + WIN_W)
    blk_mask = jnp.asarray(np.broadcast_to(col_mask[:, None, :], (GRID_W, kh, GRID_W)).reshape(GRID_W, n_loc))
    dc_idx = np.clip(kcol - qcol, -(WIN_W - 1), WIN_W - 1) + (WIN_W - 1)
    rpb_cols = rpb[:, :, dc_idx]

    def row_block(args):
        q_r, rs, r = args
        k_blk = lax.dynamic_slice_in_dim(k_grid, rs, kh, axis=1).reshape(bsz, n_loc, NA_HEADS, NA_HEAD_DIM)
        v_blk = lax.dynamic_slice_in_dim(v_grid, rs, kh, axis=1).reshape(bsz, n_loc, NA_HEADS, NA_HEAD_DIM)
        dr_idx = rs + jnp.arange(kh, dtype=jnp.int32) - r + (WIN_H - 1)
        bias = rpb_cols[:, dr_idx].transpose(0, 2, 1, 3).reshape(NA_HEADS, GRID_W, n_loc)
        s_loc = jnp.einsum("bqhd,bkhd->bhqk", q_r, k_blk).astype(jnp.float32) + bias.astype(jnp.float32)
        s_loc = jnp.where(blk_mask, s_loc, -jnp.inf)
        s_ctx = jnp.einsum("bqhd,bkhd->bhqk", q_r, kc).astype(jnp.float32)
        p = jax.nn.softmax(jnp.concatenate([s_loc, s_ctx], axis=-1), axis=-1).astype(v.dtype)
        return (jnp.einsum("bhqk,bkhd->bqhd", p[..., :n_loc], v_blk)
                + jnp.einsum("bhqk,bkhd->bqhd", p[..., n_loc:], vc))

    o = lax.map(row_block, (q_rows, row_start, r_idx))
    o = o.transpose(1, 0, 2, 3, 4).reshape(bsz, length, d) @ w_o
    oc = None
    if ctx_out:
        sc = jnp.einsum("bqhd,bkhd->bhqk", qc, kc).astype(jnp.float32)
        pc = jax.nn.softmax(sc, axis=-1).astype(vc.dtype)
        oc = jnp.einsum("bhqk,bkhd->bqhd", pc, vc).reshape(hc.shape[0], hc.shape[1], d) @ w_o
    return o, oc


def setup_inputs(seed: int = 0) -> dict:
    key = jax.random.key(seed)
    ks = jax.random.split(key, 32)
    D, F = D_MODEL, D_FF

    def nrm(k, shape, scale):
        return jax.random.normal(k, shape, jnp.float32) * scale

    def gain(k, shape):
        return 1.0 + nrm(k, shape, 0.02)

    return {
        "x": nrm(ks[0], (BATCH, SEQ, D), 1.0),
        "c": nrm(ks[1], (BATCH, D), 1.0),
        "ctx": nrm(ks[2], (BATCH, CTX_LEN, D), 1.0),
        "c_ctx": nrm(ks[3], (D,), 1.0),
        "mod_w": nrm(ks[4], (DEPTH, D, 6 * D), 0.5 * D ** -0.5),
        "mod_b": nrm(ks[5], (DEPTH, 6 * D), 0.02),
        "norm1_g": gain(ks[6], (DEPTH, D)),
        "norm2_g": gain(ks[7], (DEPTH, D)),
        "a_w_pw1": nrm(ks[8], (N_A, D, 2 * D), D ** -0.5),
        "a_b_pw1": nrm(ks[9], (N_A, 2 * D), 0.02),
        "a_w_dw": nrm(ks[10], (N_A, CONV_WIDTH, D), CONV_WIDTH ** -0.5),
        "a_b_dw": nrm(ks[11], (N_A, D), 0.02),
        "a_ln_g": gain(ks[12], (N_A, D)),
        "a_ln_b": nrm(ks[13], (N_A, D), 0.02),
        "a_w_pw2": nrm(ks[14], (N_A, D, D), D ** -0.5),
        "a_b_pw2": nrm(ks[15], (N_A, D), 0.02),
        "b_w_out": nrm(ks[16], (N_B, D, D), D ** -0.5),
        "b_b_out": nrm(ks[17], (N_B, D), 0.02),
        "c_w_qkv": nrm(ks[18], (N_C, D, 3 * D), D ** -0.5),
        "c_q_g": gain(ks[19], (N_C, NA_HEAD_DIM)),
        "c_k_g": gain(ks[20], (N_C, NA_HEAD_DIM)),
        "c_rpb": nrm(ks[21], (N_C, NA_HEADS, 2 * WIN_H - 1, 2 * WIN_W - 1), 0.1),
        "c_w_o": nrm(ks[22], (N_C, D, D), D ** -0.5),
        "f_w_up": nrm(ks[23], (DEPTH, D, 2 * F), D ** -0.5),
        "f_w_dw": nrm(ks[24], (DEPTH, FFN_CONV_WIDTH, 2 * F), FFN_CONV_WIDTH ** -0.5),
        "f_b_dw": nrm(ks[25], (DEPTH, 2 * F), 0.02),
        "f_w_down": nrm(ks[26], (DEPTH, F, D), F ** -0.5),
    }


def reference(x, c, ctx, c_ctx, mod_w, mod_b, norm1_g, norm2_g,
              a_w_pw1, a_b_pw1, a_w_dw, a_b_dw, a_ln_g, a_ln_b, a_w_pw2, a_b_pw2,
              b_w_out, b_b_out, c_w_qkv, c_q_g, c_k_g, c_rpb, c_w_o,
              f_w_up, f_w_dw, f_b_dw, f_w_down):
    silu_c = jax.nn.silu(c)
    silu_cc = jax.nn.silu(c_ctx)
    for i in range(DEPTH):
        kind = i % N_MIXERS
        j = i // N_MIXERS
        ctx_out = any(l % N_MIXERS == 2 for l in range(i + 1, DEPTH))
        ctx_in = ctx_out or kind == 2

        mod = silu_c @ mod_w[i] + mod_b[i]
        sh1, sc1, g1, sh2, sc2, g2 = jnp.split(mod[:, None, :], 6, axis=-1)
        h = modulate(rms_norm(x, norm1_g[i]), sh1, sc1)
        hc = None
        if ctx_in:
            modc = silu_cc @ mod_w[i] + mod_b[i]
            shc1, scc1, gc1, shc2, scc2, gc2 = jnp.split(modc, 6, axis=-1)
            hc = modulate(rms_norm(ctx, norm1_g[i]), shc1, scc1)

        yc = None
        if kind == 0:
            cp = (a_w_pw1[j], a_b_pw1[j], a_w_dw[j], a_b_dw[j], a_ln_g[j], a_ln_b[j], a_w_pw2[j], a_b_pw2[j])
            y = conformer_conv(h, *cp)
            if ctx_out:
                yc = conformer_conv(hc, *cp)
        elif kind == 1:
            y = fourier_mix(h, b_w_out[j], b_b_out[j])
            if ctx_out:
                yc = fourier_mix(hc, b_w_out[j], b_b_out[j])
        else:
            y, yc = neighbourhood_attention(h, hc, c_w_qkv[j], c_q_g[j], c_k_g[j], c_rpb[j], c_w_o[j], ctx_out)

        fp = (f_w_up[i], f_w_dw[i], f_b_dw[i], f_w_down[i])
        x = x + g1 * y
        x = x + g2 * conv_ffn(modulate(rms_norm(x, norm2_g[i]), sh2, sc2), *fp)
        if ctx_out:
            ctx = ctx + gc1 * yc
            ctx = ctx + gc2 * conv_ffn(modulate(rms_norm(ctx, norm2_g[i]), shc2, scc2), *fp)
    return x
```

```python
import functools
import math

import jax
import jax.numpy as jnp
from jax import lax
from jax.experimental import pallas as pl
from jax.experimental.pallas import tpu as pltpu

GRID_W = 64
FNET_GROUPS = 8
NA_HEADS = 16
WIN_H = 8
WIN_W = 16
EPS = 1e-6

F32 = jnp.float32
BF16 = jnp.bfloat16
MASKED = -1e30
V7X_VMEM_BYTES = 64 * 1024 * 1024
VMEM_LIMIT = V7X_VMEM_BYTES - 6 * 1024 * 1024
BF16_ROWS = 16
MOD_ROWS = 32


def _params(*sem):
    return pltpu.CompilerParams(dimension_semantics=sem, vmem_limit_bytes=VMEM_LIMIT)


def _dot(a, b):
    return jnp.dot(a, b, preferred_element_type=F32)


def _silu(t):
    return t * jax.nn.sigmoid(t)


def _rms_mod(x, g, shift, scale):
    y = x * lax.rsqrt(jnp.mean(x * x, axis=-1, keepdims=True) + EPS) * g
    return y * (1.0 + scale) + shift


def _fit(n, pref, unit=128):
    t = min(pref, n) // unit * unit
    while n % t:
        t -= unit
    return t


def _bsel(n):
    return (lambda b: b) if n > 1 else (lambda b: 0)


def _mod_kernel(c_ref, w_ref, b_ref, o_ref):
    s = _silu(c_ref[...]).astype(BF16)
    o_ref[0] = _dot(s, w_ref[0].astype(BF16)) + b_ref[0]


def _modulation(cond, mod_w, mod_b, tn=1024):
    depth, d, n = mod_w.shape
    rows = cond.shape[0]
    tn = _fit(n, tn)
    return pl.pallas_call(
        _mod_kernel,
        grid=(depth, n // tn),
        in_specs=[pl.BlockSpec((rows, d), lambda l, j: (0, 0)),
                  pl.BlockSpec((1, d, tn), lambda l, j: (l, 0, j)),
                  pl.BlockSpec((1, 1, tn), lambda l, j: (l, 0, j))],
        out_specs=pl.BlockSpec((1, rows, tn), lambda l, j: (l, 0, j)),
        out_shape=jax.ShapeDtypeStruct((depth, rows, n), F32),
        compiler_params=_params("parallel", "parallel"),
        name="modulation",
    )(cond, mod_w, mod_b.reshape(depth, 1, n))


def _normmod_kernel(x_ref, g_ref, sh_ref, sc_ref, o_ref):
    o_ref[0] = _rms_mod(x_ref[0], g_ref[...], sh_ref[0], sc_ref[0]).astype(BF16)


def _normmod(x, g, shift, scale, tl=512):
    b, l, d = x.shape
    tl = min(tl, l)
    sel = _bsel(shift.shape[0])
    vec = pl.BlockSpec((1, 1, d), lambda bi, i: (sel(bi), 0, 0))
    return pl.pallas_call(
        _normmod_kernel,
        grid=(b, l // tl),
        in_specs=[pl.BlockSpec((1, tl, d), lambda bi, i: (bi, i, 0)),
                  pl.BlockSpec((1, d), lambda bi, i: (0, 0)), vec, vec],
        out_specs=pl.BlockSpec((1, tl, d), lambda bi, i: (bi, i, 0)),
        out_shape=jax.ShapeDtypeStruct((b, l, d), BF16),
        compiler_params=_params("parallel", "parallel"),
        name="normmod",
    )(x, g.reshape(1, d), shift, scale)


def _mm_kernel(a_ref, w_ref, o_ref):
    o_ref[0] = _dot(a_ref[0], w_ref[...]).astype(o_ref.dtype)


def _glu_kernel(a_ref, wa_ref, wg_ref, ba_ref, bg_ref, o_ref):
    a = a_ref[0]
    ua = _dot(a, wa_ref[...]) + ba_ref[...]
    ug = _dot(a, wg_ref[...]) + bg_ref[...]
    o_ref[0] = (ua * jax.nn.sigmoid(ug)).astype(o_ref.dtype)


def _matmul(a, w, tm=2048, tn=512):
    b, l, k = a.shape
    n = w.shape[1]
    tm, tn = min(tm, l), _fit(n, tn)
    return pl.pallas_call(
        _mm_kernel,
        grid=(b, l // tm, n // tn),
        in_specs=[pl.BlockSpec((1, tm, k), lambda bi, i, j: (bi, i, 0)),
                  pl.BlockSpec((k, tn), lambda bi, i, j: (0, j))],
        out_specs=pl.BlockSpec((1, tm, tn), lambda bi, i, j: (bi, i, j)),
        out_shape=jax.ShapeDtypeStruct((b, l, n), BF16),
        compiler_params=_params("parallel", "parallel", "arbitrary"),
        name="matmul",
    )(a, w)


def _matmul_glu(a, w, bias, tm=2048, tn=512):
    b, l, k = a.shape
    n = w.shape[1] // 2
    tm, tn = min(tm, l), _fit(n, tn)
    nj = n // tn
    bias = bias.reshape(1, 2 * n)
    return pl.pallas_call(
        _glu_kernel,
        grid=(b, l // tm, nj),
        in_specs=[pl.BlockSpec((1, tm, k), lambda bi, i, j: (bi, i, 0)),
                  pl.BlockSpec((k, tn), lambda bi, i, j: (0, j)),
                  pl.BlockSpec((k, tn), lambda bi, i, j: (0, j + nj)),
                  pl.BlockSpec((1, tn), lambda bi, i, j: (0, j)),
                  pl.BlockSpec((1, tn), lambda bi, i, j: (0, j + nj))],
        out_specs=pl.BlockSpec((1, tm, tn), lambda bi, i, j: (bi, i, j)),
        out_shape=jax.ShapeDtypeStruct((b, l, n), BF16),
        compiler_params=_params("parallel", "parallel", "arbitrary"),
        name="matmul_glu",
    )(a, w, w, bias, bias)


def _gated_kernel(*refs, ln, emit_h):
    it = iter(refs)
    a_ref, w_ref, bias_ref, res_ref, gate_ref = (next(it) for _ in range(5))
    if ln:
        lng_ref, lnb_ref = next(it), next(it)
    if emit_h:
        ng_ref, nsh_ref, nsc_ref = next(it), next(it), next(it)
    o_ref = next(it)
    a = a_ref[0]
    if ln:
        t = a.astype(F32)
        mu = jnp.mean(t, axis=-1, keepdims=True)
        var = jnp.mean(jnp.square(t - mu), axis=-1, keepdims=True)
        t = (t - mu) * lax.rsqrt(var + EPS) * lng_ref[...] + lnb_ref[...]
        a = _silu(t).astype(BF16)
    xn = res_ref[0] + gate_ref[0] * (_dot(a, w_ref[...]) + bias_ref[...])
    o_ref[0] = xn
    if emit_h:
        next(it)[0] = _rms_mod(xn, ng_ref[...], nsh_ref[0], nsc_ref[0]).astype(BF16)


def _gated_matmul(a, w, bias, res, gate, ln=None, nxt=None, tm=256):
    b, l, k = a.shape
    d = w.shape[1]
    tm = min(tm, l)
    row = lambda bi, i: (bi, i, 0)
    const = lambda bi, i: (0, 0)

    def vec(arr):
        sel = _bsel(arr.shape[0])
        return pl.BlockSpec((1, 1, d), lambda bi, i: (sel(bi), 0, 0))

    args = [a, w, bias.reshape(1, d), res, gate]
    specs = [pl.BlockSpec((1, tm, k), row),
             pl.BlockSpec((k, d), const, pipeline_mode=pl.Buffered(1)),
             pl.BlockSpec((1, d), const), pl.BlockSpec((1, tm, d), row), vec(gate)]
    if ln is not None:
        args += [ln[0].reshape(1, k), ln[1].reshape(1, k)]
        specs += [pl.BlockSpec((1, k), const)] * 2
    out_shape = [jax.ShapeDtypeStruct((b, l, d), F32)]
    out_specs = [pl.BlockSpec((1, tm, d), row)]
    if nxt is not None:
        args += [nxt[0].reshape(1, d), nxt[1], nxt[2]]
        specs += [pl.BlockSpec((1, d), const), vec(nxt[1]), vec(nxt[2])]
        out_shape.append(jax.ShapeDtypeStruct((b, l, d), BF16))
        out_specs.append(pl.BlockSpec((1, tm, d), row))
    out = pl.pallas_call(
        functools.partial(_gated_kernel, ln=ln is not None, emit_h=nxt is not None),
        grid=(b, l // tm), in_specs=specs, out_specs=out_specs, out_shape=out_shape,
        compiler_params=_params("parallel", "parallel"),
        name="gated_matmul",
    )(*args)
    return (out[0], out[1]) if nxt is not None else (out[0], None)


def _dwconv_kernel(x_ref, w_ref, b_ref, o_ref, pad_ref, *, width, rt):
    l, tc = x_ref.shape[1], x_ref.shape[2]
    halo = BF16_ROWS
    first = halo - width // 2
    pad_ref[0:halo, :] = jnp.zeros((halo, tc), F32)
    pad_ref[halo + l:halo + l + halo, :] = jnp.zeros((halo, tc), F32)
    pad_ref[halo:halo + l, :] = x_ref[0].astype(F32)

    def body(t, carry):
        base = pl.multiple_of(t * rt, rt)
        win = pad_ref[pl.ds(base, rt + 2 * halo), :]
        acc = jnp.zeros((rt, tc), F32) + b_ref[...]
        for k in range(width):
            acc = acc + win[first + k:first + k + rt, :] * w_ref[k:k + 1, :]
        o_ref[0, pl.ds(base, rt), :] = acc.astype(o_ref.dtype)
        return carry

    lax.fori_loop(0, l // rt, body, 0)


def _dwconv(x, w, bias, tc=128, rt=64):
    b, l, c = x.shape
    width = w.shape[0]
    assert width // 2 <= BF16_ROWS
    return pl.pallas_call(
        functools.partial(_dwconv_kernel, width=width, rt=rt),
        grid=(b, c // tc),
        in_specs=[pl.BlockSpec((1, l, tc), lambda bi, j: (bi, 0, j)),
                  pl.BlockSpec((width, tc), lambda bi, j: (0, j)),
                  pl.BlockSpec((1, tc), lambda bi, j: (0, j))],
        out_specs=pl.BlockSpec((1, l, tc), lambda bi, j: (bi, 0, j)),
        out_shape=jax.ShapeDtypeStruct((b, l, c), BF16),
        scratch_shapes=[pltpu.VMEM((l + 2 * BF16_ROWS, tc), F32)],
        compiler_params=_params("parallel", "parallel"),
        name="dwconv",
    )(x, w, bias.reshape(1, c))


def _ffn_tiles(l, tmr):
    tmr = min(tmr, l)
    nt = l // tmr
    halo = BF16_ROWS
    m = tmr if nt == 1 else tmr + (halo if nt == 2 else 2 * halo)
    return tmr, nt, m, halo


def _ffn_kernel(h_ref, x_ref, wup_ref, wdw_ref, bdw_ref, wdn_ref, gate_ref, o_ref, act_ref,
                *, l, tmr, nt, m, halo, fc, nc, rc):
    i = pl.program_id(1)
    c = pl.program_id(2)
    if nt == 1:
        start, off = 0, 0
    else:
        start = pl.multiple_of(jnp.clip(i * tmr - halo, 0, l - m), BF16_ROWS)
        off = pl.multiple_of(i * tmr - start, BF16_ROWS)

    @pl.when(c == 0)
    def _():
        o_ref[...] = jnp.zeros_like(o_ref)

    u = _dot(h_ref[0, pl.ds(start, m), :], wup_ref[...])
    rows = lax.broadcasted_iota(jnp.int32, u.shape, 0)
    prev = jnp.where(rows == 0, 0.0, pltpu.roll(u, 1, 0))
    nxt = jnp.where(rows == m - 1, 0.0, pltpu.roll(u, m - 1, 0))
    cv = prev * wdw_ref[0:1, :] + u * wdw_ref[1:2, :] + nxt * wdw_ref[2:3, :] + bdw_ref[...]
    act_ref[...] = (_silu(cv[:, fc:]) * cv[:, :fc]).astype(BF16)
    o_ref[0] += _dot(act_ref[pl.ds(off, tmr), :], wdn_ref[...])

    @pl.when(c == nc - 1)
    def _():
        def rows_body(t, carry):
            r = pl.multiple_of(t * rc, rc)
            o_ref[0, pl.ds(r, rc), :] = x_ref[0, pl.ds(r, rc), :] + gate_ref[0] * o_ref[0, pl.ds(r, rc), :]
            return carry

        lax.fori_loop(0, tmr // rc, rows_body, 0)


FFN_CHUNK = 512


def _conv_ffn(h, x, wup, wdw, bdw, wdn, gate, tmr=512, fc=FFN_CHUNK, rc=128):
    b, l, d = h.shape
    f = wdn.shape[0]
    nc = f // fc
    tmr, nt, m, halo = _ffn_tiles(l, tmr)
    sel = _bsel(gate.shape[0])
    return pl.pallas_call(
        functools.partial(_ffn_kernel, l=l, tmr=tmr, nt=nt, m=m, halo=halo, fc=fc, nc=nc, rc=min(rc, tmr)),
        grid=(b, nt, nc),
        in_specs=[pl.BlockSpec((1, l, d), lambda bi, i, c: (bi, 0, 0), pipeline_mode=pl.Buffered(1)),
                  pl.BlockSpec((1, tmr, d), lambda bi, i, c: (bi, i, 0), pipeline_mode=pl.Buffered(1)),
                  pl.BlockSpec((d, 2 * fc), lambda bi, i, c: (0, c)),
                  pl.BlockSpec((3, 2 * fc), lambda bi, i, c: (0, c)),
                  pl.BlockSpec((1, 2 * fc), lambda bi, i, c: (0, c)),
                  pl.BlockSpec((fc, d), lambda bi, i, c: (c, 0)),
                  pl.BlockSpec((1, 1, d), lambda bi, i, c: (sel(bi), 0, 0))],
        out_specs=pl.BlockSpec((1, tmr, d), lambda bi, i, c: (bi, i, 0)),
        out_shape=jax.ShapeDtypeStruct((b, l, d), F32),
        scratch_shapes=[pltpu.VMEM((m, fc), BF16)],
        compiler_params=_params("parallel", "parallel", "arbitrary"),
        name="conv_ffn",
    )(h, x, wup, wdw, bdw, wdn, gate)


def _chunk_pairs(t, f, fc):
    lead = t.shape[:-1]
    t = t.reshape(lead + (2, f // fc, fc))
    return jnp.swapaxes(t, -3, -2).reshape(lead + (2 * f,))


def _fnet_kernel(h_ref, cc_ref, sc_ref, csl_ref, o_ref, pq_ref, *, groups, gw):
    l = h_ref.shape[1]

    @pl.when(pl.program_id(1) == 0)
    def _():
        for g in range(groups):
            hg = h_ref[0, :, g * gw:(g + 1) * gw]
            pq_ref[0:l, g * gw:(g + 1) * gw] = _dot(hg, cc_ref[...]).astype(BF16)
            pq_ref[l:2 * l, g * gw:(g + 1) * gw] = _dot(hg, sc_ref[...]).astype(BF16)

    o_ref[0] = _dot(csl_ref[...], pq_ref[...]).astype(o_ref.dtype)


def _dft_tables(n):
    idx = jnp.arange(n, dtype=jnp.int32)
    ang = ((idx[:, None] * idx[None, :]) % n).astype(F32) * (2.0 * math.pi / n)
    s = 1.0 / math.sqrt(n)
    return jnp.cos(ang) * s, jnp.sin(ang) * s


def _fourier_mix(h, tl=512):
    b, l, d = h.shape
    gw = d // FNET_GROUPS
    tl = min(tl, l)
    cc, sc = _dft_tables(gw)
    cl, sl = _dft_tables(l)
    csl = jnp.concatenate([cl, -sl], axis=1).astype(BF16)
    return pl.pallas_call(
        functools.partial(_fnet_kernel, groups=FNET_GROUPS, gw=gw),
        grid=(b, l // tl),
        in_specs=[pl.BlockSpec((1, l, d), lambda bi, i: (bi, 0, 0)),
                  pl.BlockSpec((gw, gw), lambda bi, i: (0, 0)),
                  pl.BlockSpec((gw, gw), lambda bi, i: (0, 0)),
                  pl.BlockSpec((tl, 2 * l), lambda bi, i: (i, 0))],
        out_specs=pl.BlockSpec((1, tl, d), lambda bi, i: (bi, i, 0)),
        out_shape=jax.ShapeDtypeStruct((b, l, d), BF16),
        scratch_shapes=[pltpu.VMEM((2 * l, d), BF16)],
        compiler_params=_params("parallel", "arbitrary"),
        name="fourier_mix",
    )(h, cc.astype(BF16), sc.astype(BF16), csl)


NA_QROWS = 4
NA_KROWS = 12


def _na_plan(rows):
    assert rows % NA_QROWS == 0 and rows >= NA_KROWS
    patterns, blocks = [], []
    for j in range(rows // NA_QROWS):
        r0 = j * NA_QROWS
        ks = min(max(r0 - WIN_H // 2, 0), rows - NA_KROWS)
        pat = []
        for ql in range(NA_QROWS):
            r = r0 + ql
            rs = min(max(r - WIN_H // 2, 0), rows - WIN_H)
            assert ks <= rs and rs + WIN_H <= ks + NA_KROWS
            pat.append((rs - ks, ks - r + WIN_H - 1))
        pat = tuple(pat)
        if pat not in patterns:
            patterns.append(pat)
        blocks.append((ks, patterns.index(pat)))
    return tuple(blocks), tuple(patterns)


def _na_kernel(rpb_ref, q_ref, k_ref, v_ref, kc_ref, vc_ref, qg_ref, kg_ref, o_ref,
               base_ref, tab_ref, qn_ref, kn_ref, kcn_ref, *, blocks, patterns, hd):
    w = GRID_W
    ndr, ndc = 2 * WIN_H - 1, 2 * WIN_W - 1
    head = pl.program_id(0)

    @pl.when(pl.program_id(1) == 0)
    def _build_bias_tables():
        qi = lax.broadcasted_iota(jnp.int32, (w, w), 0)
        ki = lax.broadcasted_iota(jnp.int32, (w, w), 1)
        cstart = jnp.clip(qi - WIN_W // 2, 0, w - WIN_W)
        in_window = (ki >= cstart) & (ki < cstart + WIN_W)
        dci = jnp.clip(ki - qi, -(WIN_W - 1), WIN_W - 1) + (WIN_W - 1)
        for dr in range(ndr):
            t = jnp.zeros((w, w), F32)
            for dc in range(ndc):
                t = jnp.where(dci == dc, rpb_ref[(head * ndr + dr) * ndc + dc], t)
            base_ref[dr] = jnp.where(in_window, t, MASKED)
        for p, pat in enumerate(patterns):
            for ql, (first, droff) in enumerate(pat):
                for kl in range(NA_KROWS):
                    if first <= kl < first + WIN_H:
                        piece = base_ref[kl + droff]
                    else:
                        piece = jnp.full((w, w), MASKED, F32)
                    tab_ref[p, ql * w:(ql + 1) * w, kl * w:(kl + 1) * w] = piece

    def rms(t, g):
        t = t.astype(F32)
        return t * lax.rsqrt(jnp.mean(t * t, axis=-1, keepdims=True) + EPS) * g

    qn_ref[...] = (rms(q_ref[0], qg_ref[...]) * (hd ** -0.5)).astype(BF16)
    kn_ref[...] = rms(k_ref[0], kg_ref[...]).astype(BF16)
    kcn_ref[...] = rms(kc_ref[0], kg_ref[...]).astype(BF16)

    nt = (((1,), (1,)), ((), ()))
    nq, nk = NA_QROWS * w, NA_KROWS * w
    for j, (ks, p) in enumerate(blocks):
        qb = qn_ref[j * nq:(j + 1) * nq, :]
        s1 = lax.dot_general(qb, kn_ref[ks * w:ks * w + nk, :], nt, preferred_element_type=F32) + tab_ref[p]
        s2 = lax.dot_general(qb, kcn_ref[...], nt, preferred_element_type=F32)
        mx = jnp.maximum(jnp.max(s1, axis=-1, keepdims=True), jnp.max(s2, axis=-1, keepdims=True))
        p1 = jnp.exp(s1 - mx)
        p2 = jnp.exp(s2 - mx)
        den = jnp.sum(p1, axis=-1, keepdims=True) + jnp.sum(p2, axis=-1, keepdims=True)
        o = _dot(p1.astype(BF16), v_ref[0, ks * w:ks * w + nk, :]) + _dot(p2.astype(BF16), vc_ref[0])
        o_ref[0, j * nq:(j + 1) * nq, :] = (o / den).astype(o_ref.dtype)


def _neighbourhood_attention(qkv, qkvc, q_g, k_g, rpb):
    b, l, d3 = qkv.shape
    d = d3 // 3
    nctx = qkvc.shape[1]
    hd = d // NA_HEADS
    rows = l // GRID_W
    blocks, patterns = _na_plan(rows)
    nq, nk = NA_QROWS * GRID_W, NA_KROWS * GRID_W
    seq = lambda col0: pl.BlockSpec((1, l, hd), lambda h, bi: (bi, 0, col0 + h))
    ctx = lambda col0: pl.BlockSpec((1, nctx, hd), lambda h, bi: (bi, 0, col0 + h))
    gain = pl.BlockSpec((1, hd), lambda h, bi: (0, 0))
    return pl.pallas_call(
        functools.partial(_na_kernel, blocks=blocks, patterns=patterns, hd=hd),
        grid=(NA_HEADS, b),
        in_specs=[pl.BlockSpec(memory_space=pltpu.SMEM),
                  seq(0), seq(NA_HEADS), seq(2 * NA_HEADS), ctx(NA_HEADS), ctx(2 * NA_HEADS), gain, gain],
        out_specs=pl.BlockSpec((1, l, hd), lambda h, bi: (bi, 0, h)),
        out_shape=jax.ShapeDtypeStruct((b, l, d), BF16),
        scratch_shapes=[pltpu.VMEM((2 * WIN_H - 1, GRID_W, GRID_W), F32),
                        pltpu.VMEM((len(patterns), nq, nk), F32),
                        pltpu.VMEM((l, hd), BF16), pltpu.VMEM((l, hd), BF16), pltpu.VMEM((nctx, hd), BF16)],
        compiler_params=_params("arbitrary", "arbitrary"),
        name="neighbourhood_attention",
    )(rpb.reshape(-1), qkv, qkv, qkv, qkvc, qkvc, q_g.reshape(1, hd), k_g.reshape(1, hd))


def kernel(x, c, ctx, c_ctx, mod_w, mod_b, norm1_g, norm2_g, a_w_pw1, a_b_pw1, a_w_dw, a_b_dw, a_ln_g, a_ln_b,
           a_w_pw2, a_b_pw2, b_w_out, b_b_out, c_w_qkv, c_q_g, c_k_g, c_rpb, c_w_o, f_w_up, f_w_dw, f_b_dw, f_w_down):
    bsz, seq, d = x.shape
    nctx = ctx.shape[1]
    depth = mod_w.shape[0]
    f = f_w_down.shape[1]
    fc = FFN_CHUNK
    n_mix = 3

    cond = jnp.zeros((MOD_ROWS, d), F32).at[:bsz].set(c).at[bsz].set(c_ctx)
    mods = _modulation(cond, mod_w, mod_b)

    def mod_vecs(i, first_row, n_rows):
        m = mods[i, first_row:first_row + n_rows]
        return [m[:, None, k * d:(k + 1) * d] for k in range(6)]

    ctx_out = [any(l % n_mix == 2 for l in range(i + 1, depth)) for i in range(depth)]
    ctx_in = [ctx_out[i] or i % n_mix == 2 for i in range(depth)]

    flat = lambda t: t.reshape(1, bsz * nctx, t.shape[-1])
    unflat = lambda t: t.reshape(bsz, nctx, t.shape[-1])

    for i in range(depth):
        kind, j = i % n_mix, i // n_mix
        mx = mod_vecs(i, 0, bsz)
        mc = mod_vecs(i, bsz, 1)
        h = _normmod(x, norm1_g[i], mx[0], mx[1])
        hc = _normmod(ctx, norm1_g[i], mc[0], mc[1]) if ctx_in[i] else None

        if kind == 0:
            w1, w2 = a_w_pw1[j].astype(BF16), a_w_pw2[j].astype(BF16)

            def mixer(hh, res, gate, nxt, shared):
                fl, un = (flat, unflat) if shared else ((lambda t: t), (lambda t: t))
                u = un(_matmul_glu(fl(hh), w1, a_b_pw1[j]))
                u = _dwconv(u, a_w_dw[j], a_b_dw[j])
                xo, ho = _gated_matmul(fl(u), w2, a_b_pw2[j], fl(res), gate, ln=(a_ln_g[j], a_ln_b[j]), nxt=nxt)
                return un(xo), un(ho)
        elif kind == 1:
            wo = b_w_out[j].astype(BF16)

            def mixer(hh, res, gate, nxt, shared):
                fl, un = (flat, unflat) if shared else ((lambda t: t), (lambda t: t))
                xo, ho = _gated_matmul(fl(_fourier_mix(hh)), wo, b_b_out[j], fl(res), gate, nxt=nxt)
                return un(xo), un(ho)
        else:
            wqkv, wo = c_w_qkv[j].astype(BF16), c_w_o[j].astype(BF16)
            qkvc = unflat(_matmul(flat(hc), wqkv))

            def mixer(hh, res, gate, nxt, shared):
                assert not shared, "context outputs of the attention mixer are never consumed"
                o = _neighbourhood_attention(_matmul(hh, wqkv), qkvc, c_q_g[j], c_k_g[j], c_rpb[j])
                return _gated_matmul(o, wo, jnp.zeros((d,), F32), res, gate, nxt=nxt)

        wup = _chunk_pairs(f_w_up[i], f, fc).astype(BF16)
        wdw = _chunk_pairs(f_w_dw[i], f, fc)
        bdw = _chunk_pairs(f_b_dw[i], f, fc).reshape(1, 2 * f)
        wdn = f_w_down[i].astype(BF16)

        x, h2 = mixer(h, x, mx[2], (norm2_g[i], mx[3], mx[4]), False)
        x = _conv_ffn(h2, x, wup, wdw, bdw, wdn, mx[5])
        if ctx_out[i]:
            ctx, hc2 = mixer(hc, ctx, mc[2], (norm2_g[i], mc[3], mc[4]), True)
            ctx = _conv_ffn(hc2, ctx, wup, wdw, bdw, wdn, mc[5])
    return x
```

```python
import functools
import math

import jax
import jax.numpy as jnp
from jax import lax
from jax.experimental import pallas as pl
from jax.experimental.pallas import tpu as pltpu

GRID_W = 64
FNET_GROUPS = 8
NA_HEADS = 16
WIN_H = 8
WIN_W = 16
EPS = 1e-6

F32 = jnp.float32
BF16 = jnp.bfloat16
MASKED = -1e30
V7X_VMEM_BYTES = 64 * 1024 * 1024
VMEM_LIMIT = V7X_VMEM_BYTES - 6 * 1024 * 1024
BF16_ROWS = 16
LANES = 128
MOD_ROWS = 32


def _params(*sem):
    return pltpu.CompilerParams(dimension_semantics=sem, vmem_limit_bytes=VMEM_LIMIT)


def _dot(a, b):
    return jnp.dot(a, b, preferred_element_type=F32)


def _silu(t):
    return t * jax.nn.sigmoid(t)


def _rms_mod(x, g, shift, scale):
    y = x * lax.rsqrt(jnp.mean(x * x, axis=-1, keepdims=True) + EPS) * g
    return y * (1.0 + scale) + shift


def _fit(n, pref, unit=LANES):
    t = min(pref, n) // unit * unit
    while n % t:
        t -= unit
    return t


def _bsel(n):
    return (lambda b: b) if n > 1 else (lambda b: 0)


def _mod_kernel(c_ref, w_ref, b_ref, o_ref):
    s = _silu(c_ref[...]).astype(BF16)
    o_ref[0] = _dot(s, w_ref[0].astype(BF16)) + b_ref[0]


def _modulation(cond, mod_w, mod_b, tn=1024):
    depth, d, n = mod_w.shape
    rows = cond.shape[0]
    tn = _fit(n, tn)
    return pl.pallas_call(
        _mod_kernel,
        grid=(depth, n // tn),
        in_specs=[pl.BlockSpec((rows, d), lambda l, j: (0, 0)),
                  pl.BlockSpec((1, d, tn), lambda l, j: (l, 0, j)),
                  pl.BlockSpec((1, 1, tn), lambda l, j: (l, 0, j))],
        out_specs=pl.BlockSpec((1, rows, tn), lambda l, j: (l, 0, j)),
        out_shape=jax.ShapeDtypeStruct((depth, rows, n), F32),
        compiler_params=_params("parallel", "parallel"),
        name="modulation",
    )(cond, mod_w, mod_b.reshape(depth, 1, n))


def _normmod_kernel(x_ref, g_ref, sh_ref, sc_ref, o_ref):
    o_ref[0] = _rms_mod(x_ref[0], g_ref[...], sh_ref[0], sc_ref[0]).astype(BF16)


def _normmod(x, g, shift, scale, tl=512):
    b, l, d = x.shape
    tl = min(tl, l)
    sel = _bsel(shift.shape[0])
    vec = pl.BlockSpec((1, 1, d), lambda bi, i: (sel(bi), 0, 0))
    return pl.pallas_call(
        _normmod_kernel,
        grid=(b, l // tl),
        in_specs=[pl.BlockSpec((1, tl, d), lambda bi, i: (bi, i, 0)),
                  pl.BlockSpec((1, d), lambda bi, i: (0, 0)), vec, vec],
        out_specs=pl.BlockSpec((1, tl, d), lambda bi, i: (bi, i, 0)),
        out_shape=jax.ShapeDtypeStruct((b, l, d), BF16),
        compiler_params=_params("parallel", "parallel"),
        name="normmod",
    )(x, g.reshape(1, d), shift, scale)


def _mm_kernel(a_ref, w_ref, o_ref):
    o_ref[0] = _dot(a_ref[0], w_ref[...]).astype(o_ref.dtype)


def _glu_kernel(a_ref, wa_ref, wg_ref, ba_ref, bg_ref, o_ref):
    a = a_ref[0]
    ua = _dot(a, wa_ref[...]) + ba_ref[...]
    ug = _dot(a, wg_ref[...]) + bg_ref[...]
    o_ref[0] = (ua * jax.nn.sigmoid(ug)).astype(o_ref.dtype)


def _matmul(a, w, tm=2048, tn=512):
    b, l, k = a.shape
    n = w.shape[1]
    tm, tn = min(tm, l), _fit(n, tn)
    return pl.pallas_call(
        _mm_kernel,
        grid=(b, l // tm, n // tn),
        in_specs=[pl.BlockSpec((1, tm, k), lambda bi, i, j: (bi, i, 0)),
                  pl.BlockSpec((k, tn), lambda bi, i, j: (0, j))],
        out_specs=pl.BlockSpec((1, tm, tn), lambda bi, i, j: (bi, i, j)),
        out_shape=jax.ShapeDtypeStruct((b, l, n), BF16),
        compiler_params=_params("parallel", "parallel", "arbitrary"),
        name="matmul",
    )(a, w)


def _matmul_glu(a, w, bias, tm=2048, tn=512):
    b, l, k = a.shape
    n = w.shape[1] // 2
    tm, tn = min(tm, l), _fit(n, tn)
    nj = n // tn
    bias = bias.reshape(1, 2 * n)
    return pl.pallas_call(
        _glu_kernel,
        grid=(b, l // tm, nj),
        in_specs=[pl.BlockSpec((1, tm, k), lambda bi, i, j: (bi, i, 0)),
                  pl.BlockSpec((k, tn), lambda bi, i, j: (0, j)),
                  pl.BlockSpec((k, tn), lambda bi, i, j: (0, j + nj)),
                  pl.BlockSpec((1, tn), lambda bi, i, j: (0, j)),
                  pl.BlockSpec((1, tn), lambda bi, i, j: (0, j + nj))],
        out_specs=pl.BlockSpec((1, tm, tn), lambda bi, i, j: (bi, i, j)),
        out_shape=jax.ShapeDtypeStruct((b, l, n), BF16),
        compiler_params=_params("parallel", "parallel", "arbitrary"),
        name="matmul_glu",
    )(a, w, w, bias, bias)


GATED_SUB = 256


def _gated_kernel(*refs, ln, emit_h, nsub):
    it = iter(refs)
    a_ref, w_ref, bias_ref, res_ref, gate_ref = (next(it) for _ in range(5))
    if ln:
        lng_ref, lnb_ref = next(it), next(it)
    if emit_h:
        ng_ref, nsh_ref, nsc_ref = next(it), next(it), next(it)
    o_ref = next(it)
    h_ref = next(it) if emit_h else None
    sub = a_ref.shape[1] // nsub

    def prologue(r):
        a = a_ref[0, r * sub:(r + 1) * sub, :]
        if not ln:
            return a
        t = a.astype(F32)
        mu = jnp.mean(t, axis=-1, keepdims=True)
        var = jnp.mean(jnp.square(t - mu), axis=-1, keepdims=True)
        t = (t - mu) * lax.rsqrt(var + EPS) * lng_ref[...] + lnb_ref[...]
        return _silu(t).astype(BF16)

    lhs = [prologue(r) for r in range(nsub)]
    for r in range(nsub):
        rows = slice(r * sub, (r + 1) * sub)
        xn = res_ref[0, rows, :] + gate_ref[0] * (_dot(lhs[r], w_ref[...]) + bias_ref[...])
        o_ref[0, rows, :] = xn
        if emit_h:
            h_ref[0, rows, :] = _rms_mod(xn, ng_ref[...], nsh_ref[0], nsc_ref[0]).astype(BF16)


def _gated_matmul(a, w, bias, res, gate, ln=None, nxt=None, tm=512):
    b, l, k = a.shape
    d = w.shape[1]
    tm = min(tm, l)
    nsub = max(tm // GATED_SUB, 1)
    row = lambda bi, i: (bi, i, 0)
    const = lambda bi, i: (0, 0)

    def vec(arr):
        sel = _bsel(arr.shape[0])
        return pl.BlockSpec((1, 1, d), lambda bi, i: (sel(bi), 0, 0))

    args = [a, w, bias.reshape(1, d), res, gate]
    specs = [pl.BlockSpec((1, tm, k), row),
             pl.BlockSpec((k, d), const, pipeline_mode=pl.Buffered(1)),
             pl.BlockSpec((1, d), const), pl.BlockSpec((1, tm, d), row), vec(gate)]
    if ln is not None:
        args += [ln[0].reshape(1, k), ln[1].reshape(1, k)]
        specs += [pl.BlockSpec((1, k), const)] * 2
    out_shape = [jax.ShapeDtypeStruct((b, l, d), F32)]
    out_specs = [pl.BlockSpec((1, tm, d), row)]
    if nxt is not None:
        args += [nxt[0].reshape(1, d), nxt[1], nxt[2]]
        specs += [pl.BlockSpec((1, d), const), vec(nxt[1]), vec(nxt[2])]
        out_shape.append(jax.ShapeDtypeStruct((b, l, d), BF16))
        out_specs.append(pl.BlockSpec((1, tm, d), row))
    out = pl.pallas_call(
        functools.partial(_gated_kernel, ln=ln is not None, emit_h=nxt is not None, nsub=nsub),
        grid=(b, l // tm), in_specs=specs, out_specs=out_specs, out_shape=out_shape,
        compiler_params=_params("parallel", "parallel"),
        name="gated_matmul",
    )(*args)
    return (out[0], out[1]) if nxt is not None else (out[0], None)


def _dwconv_kernel(x_ref, w_ref, b_ref, o_ref, pad_ref, *, width, rt):
    l, tc = x_ref.shape[1], x_ref.shape[2]
    halo = BF16_ROWS
    first = halo - width // 2
    pad_ref[0:halo, :] = jnp.zeros((halo, tc), F32)
    pad_ref[halo + l:halo + l + halo, :] = jnp.zeros((halo, tc), F32)
    pad_ref[halo:halo + l, :] = x_ref[0].astype(F32)

    def body(t, carry):
        base = pl.multiple_of(t * rt, rt)
        acc = jnp.zeros((rt, tc), F32) + b_ref[...]
        for k in range(width):
            acc = acc + pad_ref[pl.ds(base + first + k, rt), :] * w_ref[k:k + 1, :]
        o_ref[0, pl.ds(base, rt), :] = acc.astype(o_ref.dtype)
        return carry

    lax.fori_loop(0, l // rt, body, 0)


def _dwconv(x, w, bias, rt=64):
    b, l, c = x.shape
    width = w.shape[0]
    tc = LANES
    assert width // 2 <= BF16_ROWS
    return pl.pallas_call(
        functools.partial(_dwconv_kernel, width=width, rt=rt),
        grid=(b, c // tc),
        in_specs=[pl.BlockSpec((1, l, tc), lambda bi, j: (bi, 0, j)),
                  pl.BlockSpec((width, tc), lambda bi, j: (0, j)),
                  pl.BlockSpec((1, tc), lambda bi, j: (0, j))],
        out_specs=pl.BlockSpec((1, l, tc), lambda bi, j: (bi, 0, j)),
        out_shape=jax.ShapeDtypeStruct((b, l, c), BF16),
        scratch_shapes=[pltpu.VMEM((l + 2 * BF16_ROWS, tc), F32)],
        compiler_params=_params("parallel", "parallel"),
        name="dwconv",
    )(x, w, bias.reshape(1, c))


FFN_SUB = 256
FFN_NSUB = 2
FFN_ROWS = 1024


def _prep_up_kernel(v_ref, g_ref, o_ref, *, fs, ns):
    for s in range(ns):
        o_ref[0, 0, :, 2 * fs * s:2 * fs * s + fs] = v_ref[0, :, fs * s:fs * (s + 1)].astype(BF16)
        o_ref[0, 0, :, 2 * fs * s + fs:2 * fs * (s + 1)] = g_ref[0, :, fs * s:fs * (s + 1)].astype(BF16)


def _prep_up(w):
    depth, d, f2 = w.shape
    fs, ns = FFN_SUB, FFN_NSUB
    fc = fs * ns
    nc = f2 // 2 // fc
    return pl.pallas_call(
        functools.partial(_prep_up_kernel, fs=fs, ns=ns),
        grid=(depth, nc),
        in_specs=[pl.BlockSpec((1, d, fc), lambda l, c: (l, 0, c)),
                  pl.BlockSpec((1, d, fc), lambda l, c: (l, 0, c + nc))],
        out_specs=pl.BlockSpec((1, 1, d, 2 * fc), lambda l, c: (l, c, 0, 0)),
        out_shape=jax.ShapeDtypeStruct((depth, nc, d, 2 * fc), BF16),
        compiler_params=_params("parallel", "parallel"),
        name="prep_up",
    )(w, w)


def _chunk_pairs(t, f, fs):
    lead = t.shape[:-1]
    t = t.reshape(lead + (2, f // fs, fs))
    return jnp.swapaxes(t, -3, -2).reshape(lead + (2 * f,))


def _ffn_kernel(*refs, nt, nseg, seg, halo, fs, ns, nc, rc):
    it = iter(refs)
    if nt > 1:
        hprev_ref, hnext_ref = next(it), next(it)
    hmain_ref, x_ref, wup_ref, wdw_ref, bdw_ref, wdn_ref, gate_ref, o_ref, hp_ref, u_ref = it
    i = pl.program_id(1)
    c = pl.program_id(2)
    d = hp_ref.shape[1]
    stride = seg + halo
    lt = 2 * fs // LANES

    @pl.when(c == 0)
    def _():
        o_ref[...] = jnp.zeros_like(o_ref)
        zero = jnp.zeros((halo, d), BF16)
        for k in range(nseg + 1):
            band = zero
            if nt > 1 and k == 0:
                band = jnp.where(i > 0, hprev_ref[0], zero)
            if nt > 1 and k == nseg:
                band = jnp.where(i < nt - 1, hnext_ref[0], zero)
            hp_ref[k * stride:k * stride + halo, :] = band
        for k in range(nseg):
            hp_ref[halo + k * stride:halo + k * stride + seg, :] = hmain_ref[0, k * seg:(k + 1) * seg, :]

    hp = hp_ref[...]
    for s in range(ns):
        u = _dot(hp, wup_ref[0, :, 2 * fs * s:2 * fs * (s + 1)])
        for t in range(lt):
            u_ref[s * lt + t] = u[:, LANES * t:LANES * (t + 1)]
    for s in range(ns):
        def conv(t, k):
            col = 2 * fs * s + LANES * t
            w = wdw_ref[:, col:col + LANES]
            r0 = halo + k * stride
            ut = u_ref.at[s * lt + t]
            return (ut[r0 - 1:r0 - 1 + seg, :] * w[0:1] + ut[r0:r0 + seg, :] * w[1:2]
                    + ut[r0 + 1:r0 + 1 + seg, :] * w[2:3] + bdw_ref[:, col:col + LANES])

        act = jnp.concatenate(
            [jnp.concatenate([(_silu(conv(lt // 2 + t, k)) * conv(t, k)).astype(BF16) for t in range(lt // 2)],
                             axis=1) for k in range(nseg)], axis=0)
        o_ref[0] += _dot(act, wdn_ref[fs * s:fs * (s + 1), :])

    @pl.when(c == nc - 1)
    def _():
        def rows_body(t, carry):
            r = pl.multiple_of(t * rc, rc)
            o_ref[0, pl.ds(r, rc), :] = x_ref[0, pl.ds(r, rc), :] + gate_ref[0] * o_ref[0, pl.ds(r, rc), :]
            return carry

        lax.fori_loop(0, nseg * seg // rc, rows_body, 0)


def _conv_ffn(h, x, wup, wdw, bdw, wdn, gate, rc=128):
    b, l, d = h.shape
    nc, _, cw = wup.shape
    fs = FFN_SUB
    ns = cw // (2 * fs)
    halo = BF16_ROWS
    if l >= FFN_ROWS:
        nseg, seg, nt, nb = 1, FFN_ROWS, l // FFN_ROWS, b
    else:
        groupable = gate.shape[0] == 1
        nseg = max(n for n in range(1, b + 1) if b % n == 0 and (n == 1 or (groupable and n * l <= FFN_ROWS)))
        seg, nt, nb = l, 1, b // nseg
    rows = nseg * seg
    m = nseg * (seg + halo) + halo
    sel = _bsel(gate.shape[0])
    tile = lambda bi, i, c: (bi, i, 0)
    args, specs = [], []
    if nt > 1:
        hb, nhb = rows // halo, l // halo
        args += [h, h]
        specs += [pl.BlockSpec((1, halo, d), lambda bi, i, c: (bi, jnp.maximum(i * hb - 1, 0), 0)),
                  pl.BlockSpec((1, halo, d), lambda bi, i, c: (bi, jnp.minimum((i + 1) * hb, nhb - 1), 0))]
    args += [h.reshape(nb, nt * rows, d), x.reshape(nb, nt * rows, d), wup, wdw, bdw, wdn, gate]
    specs += [pl.BlockSpec((1, rows, d), tile, pipeline_mode=pl.Buffered(1)),
              pl.BlockSpec((1, rows, d), tile, pipeline_mode=pl.Buffered(1)),
              pl.BlockSpec((1, d, cw), lambda bi, i, c: (c, 0, 0)),
              pl.BlockSpec((3, cw), lambda bi, i, c: (0, c)),
              pl.BlockSpec((1, cw), lambda bi, i, c: (0, c)),
              pl.BlockSpec((fs * ns, d), lambda bi, i, c: (c, 0)),
              pl.BlockSpec((1, 1, d), lambda bi, i, c: (sel(bi), 0, 0))]
    out = pl.pallas_call(
        functools.partial(_ffn_kernel, nt=nt, nseg=nseg, seg=seg, halo=halo, fs=fs, ns=ns, nc=nc, rc=min(rc, rows)),
        grid=(nb, nt, nc), in_specs=specs,
        out_specs=pl.BlockSpec((1, rows, d), tile),
        out_shape=jax.ShapeDtypeStruct((nb, nt * rows, d), F32),
        scratch_shapes=[pltpu.VMEM((m, d), BF16), pltpu.VMEM((ns * 2 * fs // LANES, m, LANES), F32)],
        compiler_params=_params("parallel", "parallel", "arbitrary"),
        name="conv_ffn",
    )(*args)
    return out.reshape(b, l, d)


def _fnet_kernel(h_ref, cc_ref, sc_ref, csl_ref, o_ref, pq_ref, *, groups, gw):
    l = h_ref.shape[1]

    @pl.when(pl.program_id(1) == 0)
    def _():
        for g in range(groups):
            hg = h_ref[0, :, g * gw:(g + 1) * gw]
            pq_ref[0:l, g * gw:(g + 1) * gw] = _dot(hg, cc_ref[...]).astype(BF16)
            pq_ref[l:2 * l, g * gw:(g + 1) * gw] = _dot(hg, sc_ref[...]).astype(BF16)

    o_ref[0] = _dot(csl_ref[...], pq_ref[...]).astype(o_ref.dtype)


def _dft_tables(n):
    idx = jnp.arange(n, dtype=jnp.int32)
    ang = ((idx[:, None] * idx[None, :]) % n).astype(F32) * (2.0 * math.pi / n)
    s = 1.0 / math.sqrt(n)
    return jnp.cos(ang) * s, jnp.sin(ang) * s


def _fourier_mix(h, tl=512):
    b, l, d = h.shape
    gw = d // FNET_GROUPS
    tl = min(tl, l)
    cc, sc = _dft_tables(gw)
    cl, sl = _dft_tables(l)
    csl = jnp.concatenate([cl, -sl], axis=1).astype(BF16)
    return pl.pallas_call(
        functools.partial(_fnet_kernel, groups=FNET_GROUPS, gw=gw),
        grid=(b, l // tl),
        in_specs=[pl.BlockSpec((1, l, d), lambda bi, i: (bi, 0, 0)),
                  pl.BlockSpec((gw, gw), lambda bi, i: (0, 0)),
                  pl.BlockSpec((gw, gw), lambda bi, i: (0, 0)),
                  pl.BlockSpec((tl, 2 * l), lambda bi, i: (i, 0))],
        out_specs=pl.BlockSpec((1, tl, d), lambda bi, i: (bi, i, 0)),
        out_shape=jax.ShapeDtypeStruct((b, l, d), BF16),
        scratch_shapes=[pltpu.VMEM((2 * l, d), BF16)],
        compiler_params=_params("parallel", "arbitrary"),
        name="fourier_mix",
    )(h, cc.astype(BF16), sc.astype(BF16), csl)


NA_QROWS = 4
NA_KROWS = 12


def _na_plan(rows):
    assert rows % NA_QROWS == 0 and rows >= NA_KROWS
    patterns, blocks = [], []
    for j in range(rows // NA_QROWS):
        r0 = j * NA_QROWS
        ks = min(max(r0 - WIN_H // 2, 0), rows - NA_KROWS)
        pat = []
        for ql in range(NA_QROWS):
            r = r0 + ql
            rs = min(max(r - WIN_H // 2, 0), rows - WIN_H)
            assert ks <= rs and rs + WIN_H <= ks + NA_KROWS
            pat.append((rs - ks, ks - r + WIN_H - 1))
        pat = tuple(pat)
        if pat not in patterns:
            patterns.append(pat)
        blocks.append((ks, patterns.index(pat)))
    return tuple(blocks), tuple(patterns)


def _na_kernel(rpb_ref, q_ref, k_ref, v_ref, kc_ref, vc_ref, qg_ref, kg_ref, o_ref,
               base_ref, tab_ref, qn_ref, kn_ref, kcn_ref, *, blocks, patterns, hd):
    w = GRID_W
    ndr, ndc = 2 * WIN_H - 1, 2 * WIN_W - 1
    head = pl.program_id(0)

    @pl.when(pl.program_id(1) == 0)
    def _build_bias_tables():
        qi = lax.broadcasted_iota(jnp.int32, (w, w), 0)
        ki = lax.broadcasted_iota(jnp.int32, (w, w), 1)
        cstart = jnp.clip(qi - WIN_W // 2, 0, w - WIN_W)
        in_window = (ki >= cstart) & (ki < cstart + WIN_W)
        dci = jnp.clip(ki - qi, -(WIN_W - 1), WIN_W - 1) + (WIN_W - 1)
        for dr in range(ndr):
            t = jnp.zeros((w, w), F32)
            for dc in range(ndc):
                t = jnp.where(dci == dc, rpb_ref[(head * ndr + dr) * ndc + dc], t)
            base_ref[dr] = jnp.where(in_window, t, MASKED)
        for p, pat in enumerate(patterns):
            for ql, (first, droff) in enumerate(pat):
                for kl in range(NA_KROWS):
                    if first <= kl < first + WIN_H:
                        piece = base_ref[kl + droff]
                    else:
                        piece = jnp.full((w, w), MASKED, F32)
                    tab_ref[p, ql * w:(ql + 1) * w, kl * w:(kl + 1) * w] = piece

    def rms(t, g):
        t = t.astype(F32)
        return t * lax.rsqrt(jnp.mean(t * t, axis=-1, keepdims=True) + EPS) * g

    qn_ref[...] = (rms(q_ref[0], qg_ref[...]) * (hd ** -0.5)).astype(BF16)
    kn_ref[...] = rms(k_ref[0], kg_ref[...]).astype(BF16)
    kcn_ref[...] = rms(kc_ref[0], kg_ref[...]).astype(BF16)

    nt = (((1,), (1,)), ((), ()))
    nq, nk = NA_QROWS * w, NA_KROWS * w
    for j, (ks, p) in enumerate(blocks):
        qb = qn_ref[j * nq:(j + 1) * nq, :]
        s1 = lax.dot_general(qb, kn_ref[ks * w:ks * w + nk, :], nt, preferred_element_type=F32) + tab_ref[p]
        s2 = lax.dot_general(qb, kcn_ref[...], nt, preferred_element_type=F32)
        mx = jnp.maximum(jnp.max(s1, axis=-1, keepdims=True), jnp.max(s2, axis=-1, keepdims=True))
        p1 = jnp.exp(s1 - mx)
        p2 = jnp.exp(s2 - mx)
        den = jnp.sum(p1, axis=-1, keepdims=True) + jnp.sum(p2, axis=-1, keepdims=True)
        o = _dot(p1.astype(BF16), v_ref[0, ks * w:ks * w + nk, :]) + _dot(p2.astype(BF16), vc_ref[0])
        o_ref[0, j * nq:(j + 1) * nq, :] = (o / den).astype(o_ref.dtype)


def _neighbourhood_attention(qkv, qkvc, q_g, k_g, rpb):
    b, l, d3 = qkv.shape
    d = d3 // 3
    nctx = qkvc.shape[1]
    hd = d // NA_HEADS
    rows = l // GRID_W
    blocks, patterns = _na_plan(rows)
    nq, nk = NA_QROWS * GRID_W, NA_KROWS * GRID_W
    seq = lambda col0: pl.BlockSpec((1, l, hd), lambda h, bi: (bi, 0, col0 + h))
    ctx = lambda col0: pl.BlockSpec((1, nctx, hd), lambda h, bi: (bi, 0, col0 + h))
    gain = pl.BlockSpec((1, hd), lambda h, bi: (0, 0))
    return pl.pallas_call(
        functools.partial(_na_kernel, blocks=blocks, patterns=patterns, hd=hd),
        grid=(NA_HEADS, b),
        in_specs=[pl.BlockSpec(memory_space=pltpu.SMEM),
                  seq(0), seq(NA_HEADS), seq(2 * NA_HEADS), ctx(NA_HEADS), ctx(2 * NA_HEADS), gain, gain],
        out_specs=pl.BlockSpec((1, l, hd), lambda h, bi: (bi, 0, h)),
        out_shape=jax.ShapeDtypeStruct((b, l, d), BF16),
        scratch_shapes=[pltpu.VMEM((2 * WIN_H - 1, GRID_W, GRID_W), F32),
                        pltpu.VMEM((len(patterns), nq, nk), F32),
                        pltpu.VMEM((l, hd), BF16), pltpu.VMEM((l, hd), BF16), pltpu.VMEM((nctx, hd), BF16)],
        compiler_params=_params("arbitrary", "arbitrary"),
        name="neighbourhood_attention",
    )(rpb.reshape(-1), qkv, qkv, qkv, qkvc, qkvc, q_g.reshape(1, hd), k_g.reshape(1, hd))


def kernel(x, c, ctx, c_ctx, mod_w, mod_b, norm1_g, norm2_g, a_w_pw1, a_b_pw1, a_w_dw, a_b_dw, a_ln_g, a_ln_b,
           a_w_pw2, a_b_pw2, b_w_out, b_b_out, c_w_qkv, c_q_g, c_k_g, c_rpb, c_w_o, f_w_up, f_w_dw, f_b_dw, f_w_down):
    bsz, seq, d = x.shape
    nctx = ctx.shape[1]
    depth = mod_w.shape[0]
    f = f_w_down.shape[1]
    n_mix = 3

    cond = jnp.zeros((MOD_ROWS, d), F32).at[:bsz].set(c).at[bsz].set(c_ctx)
    mods = _modulation(cond, mod_w, mod_b)

    def mod_vecs(i, first_row, n_rows):
        m = mods[i, first_row:first_row + n_rows]
        return [m[:, None, k * d:(k + 1) * d] for k in range(6)]

    ctx_out = [any(l % n_mix == 2 for l in range(i + 1, depth)) for i in range(depth)]
    ctx_in = [ctx_out[i] or i % n_mix == 2 for i in range(depth)]

    flat = lambda t: t.reshape(1, bsz * nctx, t.shape[-1])
    unflat = lambda t: t.reshape(bsz, nctx, t.shape[-1])

    wup_all = _prep_up(f_w_up)
    wdn_all = f_w_down.astype(BF16)

    for i in range(depth):
        kind, j = i % n_mix, i // n_mix
        mx = mod_vecs(i, 0, bsz)
        mc = mod_vecs(i, bsz, 1)
        h = _normmod(x, norm1_g[i], mx[0], mx[1])
        hc = _normmod(ctx, norm1_g[i], mc[0], mc[1]) if ctx_in[i] else None

        if kind == 0:
            w1, w2 = a_w_pw1[j].astype(BF16), a_w_pw2[j].astype(BF16)

            def mixer(hh, res, gate, nxt, shared):
                fl, un = (flat, unflat) if shared else ((lambda t: t), (lambda t: t))
                u = un(_matmul_glu(fl(hh), w1, a_b_pw1[j]))
                u = _dwconv(u, a_w_dw[j], a_b_dw[j])
                xo, ho = _gated_matmul(fl(u), w2, a_b_pw2[j], fl(res), gate, ln=(a_ln_g[j], a_ln_b[j]), nxt=nxt)
                return un(xo), un(ho)
        elif kind == 1:
            wo = b_w_out[j].astype(BF16)

            def mixer(hh, res, gate, nxt, shared):
                fl, un = (flat, unflat) if shared else ((lambda t: t), (lambda t: t))
                xo, ho = _gated_matmul(fl(_fourier_mix(hh)), wo, b_b_out[j], fl(res), gate, nxt=nxt)
                return un(xo), un(ho)
        else:
            wqkv, wo = c_w_qkv[j].astype(BF16), c_w_o[j].astype(BF16)
            qkvc = unflat(_matmul(flat(hc), wqkv))

            def mixer(hh, res, gate, nxt, shared):
                assert not shared, "context outputs of the attention mixer are never consumed"
                o = _neighbourhood_attention(_matmul(hh, wqkv), qkvc, c_q_g[j], c_k_g[j], c_rpb[j])
                return _gated_matmul(o, wo, jnp.zeros((d,), F32), res, gate, nxt=nxt)

        wdw = _chunk_pairs(f_w_dw[i], f, FFN_SUB)
        bdw = _chunk_pairs(f_b_dw[i], f, FFN_SUB).reshape(1, 2 * f)

        x, h2 = mixer(h, x, mx[2], (norm2_g[i], mx[3], mx[4]), False)
        x = _conv_ffn(h2, x, wup_all[i], wdw, bdw, wdn_all[i], mx[5])
        if ctx_out[i]:
            ctx, hc2 = mixer(hc, ctx, mc[2], (norm2_g[i], mc[3], mc[4]), True)
            ctx = _conv_ffn(hc2, ctx, wup_all[i], wdw, bdw, wdn_all[i], mc[5])
    return x
```

```python
import functools
import math

import jax
import jax.numpy as jnp
from jax import lax
from jax.experimental import pallas as pl
from jax.experimental.pallas import tpu as pltpu

GRID_W = 64
FNET_GROUPS = 8
NA_HEADS = 16
WIN_H = 8
WIN_W = 16
EPS = 1e-6

F32 = jnp.float32
BF16 = jnp.bfloat16
MASKED = -1e30
V7X_VMEM_BYTES = 64 * 1024 * 1024
VMEM_LIMIT = V7X_VMEM_BYTES - 6 * 1024 * 1024
BF16_ROWS = 16
LANES = 128
MOD_ROWS = 32


def _params(*sem):
    return pltpu.CompilerParams(dimension_semantics=sem, vmem_limit_bytes=VMEM_LIMIT)


def _dot(a, b):
    return jnp.dot(a, b, preferred_element_type=F32)


def _silu(t):
    return t * jax.nn.sigmoid(t)


def _rms_mod(x, g, shift, scale):
    y = x * lax.rsqrt(jnp.mean(x * x, axis=-1, keepdims=True) + EPS) * g
    return y * (1.0 + scale) + shift


def _fit(n, pref, unit=LANES):
    t = min(pref, n) // unit * unit
    while n % t:
        t -= unit
    return t


def _bsel(n):
    return (lambda b: b) if n > 1 else (lambda b: 0)


def _mod_kernel(c_ref, w_ref, b_ref, o_ref):
    s = _silu(c_ref[...]).astype(BF16)
    o_ref[0] = _dot(s, w_ref[0].astype(BF16)) + b_ref[0]


def _modulation(cond, mod_w, mod_b, tn=1024):
    depth, d, n = mod_w.shape
    rows = cond.shape[0]
    tn = _fit(n, tn)
    return pl.pallas_call(
        _mod_kernel,
        grid=(depth, n // tn),
        in_specs=[pl.BlockSpec((rows, d), lambda l, j: (0, 0)),
                  pl.BlockSpec((1, d, tn), lambda l, j: (l, 0, j)),
                  pl.BlockSpec((1, 1, tn), lambda l, j: (l, 0, j))],
        out_specs=pl.BlockSpec((1, rows, tn), lambda l, j: (l, 0, j)),
        out_shape=jax.ShapeDtypeStruct((depth, rows, n), F32),
        compiler_params=_params("parallel", "parallel"),
        name="modulation",
    )(cond, mod_w, mod_b.reshape(depth, 1, n))


def _normmod_kernel(x_ref, g_ref, sh_ref, sc_ref, o_ref):
    o_ref[0] = _rms_mod(x_ref[0], g_ref[...], sh_ref[0], sc_ref[0]).astype(BF16)


def _normmod(x, g, shift, scale, tl=512):
    b, l, d = x.shape
    tl = min(tl, l)
    sel = _bsel(shift.shape[0])
    vec = pl.BlockSpec((1, 1, d), lambda bi, i: (sel(bi), 0, 0))
    return pl.pallas_call(
        _normmod_kernel,
        grid=(b, l // tl),
        in_specs=[pl.BlockSpec((1, tl, d), lambda bi, i: (bi, i, 0)),
                  pl.BlockSpec((1, d), lambda bi, i: (0, 0)), vec, vec],
        out_specs=pl.BlockSpec((1, tl, d), lambda bi, i: (bi, i, 0)),
        out_shape=jax.ShapeDtypeStruct((b, l, d), BF16),
        compiler_params=_params("parallel", "parallel"),
        name="normmod",
    )(x, g.reshape(1, d), shift, scale)


def _mm_kernel(a_ref, w_ref, o_ref):
    o_ref[0] = _dot(a_ref[0], w_ref[...]).astype(o_ref.dtype)


def _glu_kernel(a_ref, wa_ref, wg_ref, ba_ref, bg_ref, o_ref):
    a = a_ref[0]
    ua = _dot(a, wa_ref[...]) + ba_ref[...]
    ug = _dot(a, wg_ref[...]) + bg_ref[...]
    o_ref[0] = (ua * jax.nn.sigmoid(ug)).astype(o_ref.dtype)


def _matmul(a, w, tm=2048, tn=512):
    b, l, k = a.shape
    n = w.shape[1]
    tm, tn = min(tm, l), _fit(n, tn)
    return pl.pallas_call(
        _mm_kernel,
        grid=(b, l // tm, n // tn),
        in_specs=[pl.BlockSpec((1, tm, k), lambda bi, i, j: (bi, i, 0)),
                  pl.BlockSpec((k, tn), lambda bi, i, j: (0, j))],
        out_specs=pl.BlockSpec((1, tm, tn), lambda bi, i, j: (bi, i, j)),
        out_shape=jax.ShapeDtypeStruct((b, l, n), BF16),
        compiler_params=_params("parallel", "parallel", "arbitrary"),
        name="matmul",
    )(a, w)


def _matmul_glu(a, w, bias, tm=2048, tn=512):
    b, l, k = a.shape
    n = w.shape[1] // 2
    tm, tn = min(tm, l), _fit(n, tn)
    nj = n // tn
    bias = bias.reshape(1, 2 * n)
    return pl.pallas_call(
        _glu_kernel,
        grid=(b, l // tm, nj),
        in_specs=[pl.BlockSpec((1, tm, k), lambda bi, i, j: (bi, i, 0)),
                  pl.BlockSpec((k, tn), lambda bi, i, j: (0, j)),
                  pl.BlockSpec((k, tn), lambda bi, i, j: (0, j + nj)),
                  pl.BlockSpec((1, tn), lambda bi, i, j: (0, j)),
                  pl.BlockSpec((1, tn), lambda bi, i, j: (0, j + nj))],
        out_specs=pl.BlockSpec((1, tm, tn), lambda bi, i, j: (bi, i, j)),
        out_shape=jax.ShapeDtypeStruct((b, l, n), BF16),
        compiler_params=_params("parallel", "parallel", "arbitrary"),
        name="matmul_glu",
    )(a, w, w, bias, bias)


GATED_SUB = 256


def _gated_kernel(*refs, ln, emit_h, nsub):
    it = iter(refs)
    a_ref, w_ref, bias_ref, res_ref, gate_ref = (next(it) for _ in range(5))
    if ln:
        lng_ref, lnb_ref = next(it), next(it)
    if emit_h:
        ng_ref, nsh_ref, nsc_ref = next(it), next(it), next(it)
    o_ref = next(it)
    h_ref = next(it) if emit_h else None
    sub = a_ref.shape[1] // nsub

    def prologue(r):
        a = a_ref[0, r * sub:(r + 1) * sub, :]
        if not ln:
            return a
        t = a.astype(F32)
        mu = jnp.mean(t, axis=-1, keepdims=True)
        var = jnp.mean(jnp.square(t - mu), axis=-1, keepdims=True)
        t = (t - mu) * lax.rsqrt(var + EPS) * lng_ref[...] + lnb_ref[...]
        return _silu(t).astype(BF16)

    lhs = [prologue(r) for r in range(nsub)]
    for r in range(nsub):
        rows = slice(r * sub, (r + 1) * sub)
        xn = res_ref[0, rows, :] + gate_ref[0] * (_dot(lhs[r], w_ref[...]) + bias_ref[...])
        o_ref[0, rows, :] = xn
        if emit_h:
            h_ref[0, rows, :] = _rms_mod(xn, ng_ref[...], nsh_ref[0], nsc_ref[0]).astype(BF16)


def _gated_matmul(a, w, bias, res, gate, ln=None, nxt=None, tm=512):
    b, l, k = a.shape
    d = w.shape[1]
    tm = min(tm, l)
    nsub = max(tm // GATED_SUB, 1)
    row = lambda bi, i: (bi, i, 0)
    const = lambda bi, i: (0, 0)

    def vec(arr):
        sel = _bsel(arr.shape[0])
        return pl.BlockSpec((1, 1, d), lambda bi, i: (sel(bi), 0, 0))

    args = [a, w, bias.reshape(1, d), res, gate]
    specs = [pl.BlockSpec((1, tm, k), row),
             pl.BlockSpec((k, d), const, pipeline_mode=pl.Buffered(1)),
             pl.BlockSpec((1, d), const), pl.BlockSpec((1, tm, d), row), vec(gate)]
    if ln is not None:
        args += [ln[0].reshape(1, k), ln[1].reshape(1, k)]
        specs += [pl.BlockSpec((1, k), const)] * 2
    out_shape = [jax.ShapeDtypeStruct((b, l, d), F32)]
    out_specs = [pl.BlockSpec((1, tm, d), row)]
    if nxt is not None:
        args += [nxt[0].reshape(1, d), nxt[1], nxt[2]]
        specs += [pl.BlockSpec((1, d), const), vec(nxt[1]), vec(nxt[2])]
        out_shape.append(jax.ShapeDtypeStruct((b, l, d), BF16))
        out_specs.append(pl.BlockSpec((1, tm, d), row))
    out = pl.pallas_call(
        functools.partial(_gated_kernel, ln=ln is not None, emit_h=nxt is not None, nsub=nsub),
        grid=(b, l // tm), in_specs=specs, out_specs=out_specs, out_shape=out_shape,
        compiler_params=_params("parallel", "parallel"),
        name="gated_matmul",
    )(*args)
    return (out[0], out[1]) if nxt is not None else (out[0], None)


def _dwconv_kernel(x_ref, w_ref, b_ref, o_ref, pad_ref, *, width, rt):
    l, tc = x_ref.shape[1], x_ref.shape[2]
    halo = BF16_ROWS
    first = halo - width // 2
    pad_ref[0:halo, :] = jnp.zeros((halo, tc), F32)
    pad_ref[halo + l:halo + l + halo, :] = jnp.zeros((halo, tc), F32)
    pad_ref[halo:halo + l, :] = x_ref[0].astype(F32)

    def body(t, carry):
        base = pl.multiple_of(t * rt, rt)
        acc = jnp.zeros((rt, tc), F32) + b_ref[...]
        for k in range(width):
            acc = acc + pad_ref[pl.ds(base + first + k, rt), :] * w_ref[k:k + 1, :]
        o_ref[0, pl.ds(base, rt), :] = acc.astype(o_ref.dtype)
        return carry

    lax.fori_loop(0, l // rt, body, 0)


def _dwconv(x, w, bias, rt=64):
    b, l, c = x.shape
    width = w.shape[0]
    tc = LANES
    assert width // 2 <= BF16_ROWS
    return pl.pallas_call(
        functools.partial(_dwconv_kernel, width=width, rt=rt),
        grid=(b, c // tc),
        in_specs=[pl.BlockSpec((1, l, tc), lambda bi, j: (bi, 0, j)),
                  pl.BlockSpec((width, tc), lambda bi, j: (0, j)),
                  pl.BlockSpec((1, tc), lambda bi, j: (0, j))],
        out_specs=pl.BlockSpec((1, l, tc), lambda bi, j: (bi, 0, j)),
        out_shape=jax.ShapeDtypeStruct((b, l, c), BF16),
        scratch_shapes=[pltpu.VMEM((l + 2 * BF16_ROWS, tc), F32)],
        compiler_params=_params("parallel", "parallel"),
        name="dwconv",
    )(x, w, bias.reshape(1, c))


FFN_SUB = 256
FFN_NSUB = 2
FFN_ROWS = 1024


def _prep_up_kernel(v_ref, g_ref, o_ref, *, fs, ns):
    for s in range(ns):
        o_ref[0, 0, :, 2 * fs * s:2 * fs * s + fs] = v_ref[0, :, fs * s:fs * (s + 1)].astype(BF16)
        o_ref[0, 0, :, 2 * fs * s + fs:2 * fs * (s + 1)] = g_ref[0, :, fs * s:fs * (s + 1)].astype(BF16)


def _prep_up(w):
    depth, d, f2 = w.shape
    fs, ns = FFN_SUB, FFN_NSUB
    fc = fs * ns
    nc = f2 // 2 // fc
    return pl.pallas_call(
        functools.partial(_prep_up_kernel, fs=fs, ns=ns),
        grid=(depth, nc),
        in_specs=[pl.BlockSpec((1, d, fc), lambda l, c: (l, 0, c)),
                  pl.BlockSpec((1, d, fc), lambda l, c: (l, 0, c + nc))],
        out_specs=pl.BlockSpec((1, 1, d, 2 * fc), lambda l, c: (l, c, 0, 0)),
        out_shape=jax.ShapeDtypeStruct((depth, nc, d, 2 * fc), BF16),
        compiler_params=_params("parallel", "parallel"),
        name="prep_up",
    )(w, w)


def _chunk_pairs(t, f, fs):
    lead = t.shape[:-1]
    t = t.reshape(lead + (2, f // fs, fs))
    return jnp.swapaxes(t, -3, -2).reshape(lead + (2 * f,))


def _ffn_kernel(*refs, nt, nseg, seg, halo, fs, ns, nc, rc, emit_h):
    it = iter(refs)
    if nt > 1:
        hprev_ref, hnext_ref = next(it), next(it)
    h_hbm, x_hbm, wup_ref, wdw_ref, bdw_ref, wdn_ref, gate_ref = (next(it) for _ in range(7))
    if emit_h:
        ng_ref, nsh_ref, nsc_ref = next(it), next(it), next(it)
    o_ref = next(it)
    hn_ref = next(it) if emit_h else None
    hp_ref, u_ref, x_sem, h_sem = it
    bi = pl.program_id(0)
    i = pl.program_id(1)
    c = pl.program_id(2)
    d = hp_ref.shape[1]
    rows = nseg * seg
    stride = seg + halo
    lt = 2 * fs // LANES

    def x_copy():
        return pltpu.make_async_copy(x_hbm.at[bi, pl.ds(i * rows, rows), :], o_ref.at[0], x_sem)

    def h_copy(k):
        return pltpu.make_async_copy(h_hbm.at[bi, pl.ds(i * rows + k * seg, seg), :],
                                     hp_ref.at[pl.ds(halo + k * stride, seg), :], h_sem.at[k])

    @pl.when(c == 0)
    def _():
        x_copy().start()
        for k in range(nseg):
            h_copy(k).start()
        zero = jnp.zeros((halo, d), BF16)
        for k in range(nseg + 1):
            band = zero
            if nt > 1 and k == 0:
                band = jnp.where(i > 0, hprev_ref[0], zero)
            if nt > 1 and k == nseg:
                band = jnp.where(i < nt - 1, hnext_ref[0], zero)
            hp_ref[k * stride:k * stride + halo, :] = band
        for k in range(nseg):
            h_copy(k).wait()
        x_copy().wait()

    hp = hp_ref[...]
    for s in range(ns):
        u = _dot(hp, wup_ref[0, 0, :, 2 * fs * s:2 * fs * (s + 1)])
        for t in range(lt):
            u_ref[s * lt + t] = u[:, LANES * t:LANES * (t + 1)]
    for s in range(ns):
        def conv(t, k):
            col = 2 * fs * s + LANES * t
            w = wdw_ref[:, col:col + LANES]
            r0 = halo + k * stride
            ut = u_ref.at[s * lt + t]
            return (ut[r0 - 1:r0 - 1 + seg, :] * w[0:1] + ut[r0:r0 + seg, :] * w[1:2]
                    + ut[r0 + 1:r0 + 1 + seg, :] * w[2:3] + bdw_ref[:, col:col + LANES])

        act = jnp.concatenate(
            [jnp.concatenate([(_silu(conv(lt // 2 + t, k)) * conv(t, k)).astype(BF16) for t in range(lt // 2)],
                             axis=1) for k in range(nseg)], axis=0)
        o_ref[0] += gate_ref[0] * _dot(act, wdn_ref[0, fs * s:fs * (s + 1), :])

    if emit_h:
        @pl.when(c == nc - 1)
        def _():
            def rows_body(t, carry):
                r = pl.multiple_of(t * rc, rc)
                xn = o_ref[0, pl.ds(r, rc), :]
                hn_ref[0, pl.ds(r, rc), :] = _rms_mod(xn, ng_ref[...], nsh_ref[0], nsc_ref[0]).astype(BF16)
                return carry

            lax.fori_loop(0, rows // rc, rows_body, 0)


def _conv_ffn(h, x, layer, wup, wdw, bdw, wdn, gate, nxt=None, rc=128):
    b, l, d = h.shape
    _, nc, _, cw = wup.shape
    fs = FFN_SUB
    ns = cw // (2 * fs)
    halo = BF16_ROWS
    if l >= FFN_ROWS:
        nseg, seg, nt, nb = 1, FFN_ROWS, l // FFN_ROWS, b
    else:
        groupable = gate.shape[0] == 1
        nseg = max(n for n in range(1, b + 1) if b % n == 0 and (n == 1 or (groupable and n * l <= FFN_ROWS)))
        seg, nt, nb = l, 1, b // nseg
    rows = nseg * seg
    m = nseg * (seg + halo) + halo
    tile = lambda bi, i, c: (bi, i, 0)
    args, specs = [], []
    if nt > 1:
        hb, nhb = rows // halo, l // halo
        args += [h, h]
        specs += [pl.BlockSpec((1, halo, d), lambda bi, i, c: (bi, jnp.maximum(i * hb - 1, 0), 0)),
                  pl.BlockSpec((1, halo, d), lambda bi, i, c: (bi, jnp.minimum((i + 1) * hb, nhb - 1), 0))]
    vec = lambda arr: pl.BlockSpec((1, 1, d), lambda bi, i, c: (_bsel(arr.shape[0])(bi), 0, 0))
    args += [h.reshape(nb, nt * rows, d), x.reshape(nb, nt * rows, d), wup, wdw, bdw, wdn, gate]
    specs += [pl.BlockSpec(memory_space=pl.ANY),
              pl.BlockSpec(memory_space=pl.ANY),
              pl.BlockSpec((1, 1, d, cw), lambda bi, i, c: (layer, c, 0, 0)),
              pl.BlockSpec((3, cw), lambda bi, i, c: (0, c)),
              pl.BlockSpec((1, cw), lambda bi, i, c: (0, c)),
              pl.BlockSpec((1, fs * ns, d), lambda bi, i, c: (layer, c, 0)),
              vec(gate)]
    out_shape = [jax.ShapeDtypeStruct((nb, nt * rows, d), F32)]
    out_specs = [pl.BlockSpec((1, rows, d), tile)]
    if nxt is not None:
        args += [nxt[0].reshape(1, d), nxt[1], nxt[2]]
        specs += [pl.BlockSpec((1, d), lambda bi, i, c: (0, 0)), vec(nxt[1]), vec(nxt[2])]
        out_shape.append(jax.ShapeDtypeStruct((nb, nt * rows, d), BF16))
        out_specs.append(pl.BlockSpec((1, rows, d), tile))
    out = pl.pallas_call(
        functools.partial(_ffn_kernel, nt=nt, nseg=nseg, seg=seg, halo=halo, fs=fs, ns=ns, nc=nc, rc=min(rc, rows),
                          emit_h=nxt is not None),
        grid=(nb, nt, nc), in_specs=specs, out_specs=out_specs, out_shape=out_shape,
        scratch_shapes=[pltpu.VMEM((m, d), BF16), pltpu.VMEM((ns * 2 * fs // LANES, m, LANES), F32),
                        pltpu.SemaphoreType.DMA(()), pltpu.SemaphoreType.DMA((nseg,))],
        compiler_params=_params("parallel", "parallel", "arbitrary"),
        name="conv_ffn",
    )(*args)
    return out[0].reshape(b, l, d), (out[1].reshape(b, l, d) if nxt is not None else None)


def _fnet_kernel(h_ref, cc_ref, sc_ref, csl_ref, o_ref, pq_ref, *, groups, gw):
    l = h_ref.shape[1]

    @pl.when(pl.program_id(1) == 0)
    def _():
        for g in range(groups):
            hg = h_ref[0, :, g * gw:(g + 1) * gw]
            pq_ref[0:l, g * gw:(g + 1) * gw] = _dot(hg, cc_ref[...]).astype(BF16)
            pq_ref[l:2 * l, g * gw:(g + 1) * gw] = _dot(hg, sc_ref[...]).astype(BF16)

    o_ref[0] = _dot(csl_ref[...], pq_ref[...]).astype(o_ref.dtype)


def _dft_tables(n):
    idx = jnp.arange(n, dtype=jnp.int32)
    ang = ((idx[:, None] * idx[None, :]) % n).astype(F32) * (2.0 * math.pi / n)
    s = 1.0 / math.sqrt(n)
    return jnp.cos(ang) * s, jnp.sin(ang) * s


def _fourier_mix(h, tl=512):
    b, l, d = h.shape
    gw = d // FNET_GROUPS
    tl = min(tl, l)
    cc, sc = _dft_tables(gw)
    cl, sl = _dft_tables(l)
    csl = jnp.concatenate([cl, -sl], axis=1).astype(BF16)
    return pl.pallas_call(
        functools.partial(_fnet_kernel, groups=FNET_GROUPS, gw=gw),
        grid=(b, l // tl),
        in_specs=[pl.BlockSpec((1, l, d), lambda bi, i: (bi, 0, 0)),
                  pl.BlockSpec((gw, gw), lambda bi, i: (0, 0)),
                  pl.BlockSpec((gw, gw), lambda bi, i: (0, 0)),
                  pl.BlockSpec((tl, 2 * l), lambda bi, i: (i, 0))],
        out_specs=pl.BlockSpec((1, tl, d), lambda bi, i: (bi, i, 0)),
        out_shape=jax.ShapeDtypeStruct((b, l, d), BF16),
        scratch_shapes=[pltpu.VMEM((2 * l, d), BF16)],
        compiler_params=_params("parallel", "arbitrary"),
        name="fourier_mix",
    )(h, cc.astype(BF16), sc.astype(BF16), csl)


NA_QROWS = 4
NA_KROWS = 12


def _na_plan(rows):
    assert rows % NA_QROWS == 0 and rows >= NA_KROWS
    patterns, blocks = [], []
    for j in range(rows // NA_QROWS):
        r0 = j * NA_QROWS
        ks = min(max(r0 - WIN_H // 2, 0), rows - NA_KROWS)
        pat = []
        for ql in range(NA_QROWS):
            r = r0 + ql
            rs = min(max(r - WIN_H // 2, 0), rows - WIN_H)
            assert ks <= rs and rs + WIN_H <= ks + NA_KROWS
            pat.append((rs - ks, ks - r + WIN_H - 1))
        pat = tuple(pat)
        if pat not in patterns:
            patterns.append(pat)
        blocks.append((ks, patterns.index(pat)))
    return tuple(blocks), tuple(patterns)


def _na_kernel(rpb_ref, q_ref, k_ref, v_ref, kc_ref, vc_ref, qg_ref, kg_ref, o_ref,
               base_ref, tab_ref, qn_ref, kn_ref, kcn_ref, *, blocks, patterns, hd):
    w = GRID_W
    ndr, ndc = 2 * WIN_H - 1, 2 * WIN_W - 1
    head = pl.program_id(0)

    @pl.when(pl.program_id(1) == 0)
    def _build_bias_tables():
        qi = lax.broadcasted_iota(jnp.int32, (w, w), 0)
        ki = lax.broadcasted_iota(jnp.int32, (w, w), 1)
        cstart = jnp.clip(qi - WIN_W // 2, 0, w - WIN_W)
        in_window = (ki >= cstart) & (ki < cstart + WIN_W)
        dci = jnp.clip(ki - qi, -(WIN_W - 1), WIN_W - 1) + (WIN_W - 1)
        for dr in range(ndr):
            t = jnp.zeros((w, w), F32)
            for dc in range(ndc):
                t = jnp.where(dci == dc, rpb_ref[(head * ndr + dr) * ndc + dc], t)
            base_ref[dr] = jnp.where(in_window, t, MASKED)
        for p, pat in enumerate(patterns):
            for ql, (first, droff) in enumerate(pat):
                for kl in range(NA_KROWS):
                    if first <= kl < first + WIN_H:
                        piece = base_ref[kl + droff]
                    else:
                        piece = jnp.full((w, w), MASKED, F32)
                    tab_ref[p, ql * w:(ql + 1) * w, kl * w:(kl + 1) * w] = piece

    def rms(t, g):
        t = t.astype(F32)
        return t * lax.rsqrt(jnp.mean(t * t, axis=-1, keepdims=True) + EPS) * g

    qn_ref[...] = (rms(q_ref[0], qg_ref[...]) * (hd ** -0.5)).astype(BF16)
    kn_ref[...] = rms(k_ref[0], kg_ref[...]).astype(BF16)
    kcn_ref[...] = rms(kc_ref[0], kg_ref[...]).astype(BF16)

    nt = (((1,), (1,)), ((), ()))
    nq, nk = NA_QROWS * w, NA_KROWS * w
    for j, (ks, p) in enumerate(blocks):
        qb = qn_ref[j * nq:(j + 1) * nq, :]
        s1 = lax.dot_general(qb, kn_ref[ks * w:ks * w + nk, :], nt, preferred_element_type=F32) + tab_ref[p]
        s2 = lax.dot_general(qb, kcn_ref[...], nt, preferred_element_type=F32)
        mx = jnp.maximum(jnp.max(s1, axis=-1, keepdims=True), jnp.max(s2, axis=-1, keepdims=True))
        p1 = jnp.exp(s1 - mx)
        p2 = jnp.exp(s2 - mx)
        den = jnp.sum(p1, axis=-1, keepdims=True) + jnp.sum(p2, axis=-1, keepdims=True)
        o = _dot(p1.astype(BF16), v_ref[0, ks * w:ks * w + nk, :]) + _dot(p2.astype(BF16), vc_ref[0])
        o_ref[0, j * nq:(j + 1) * nq, :] = (o / den).astype(o_ref.dtype)


def _neighbourhood_attention(qkv, qkvc, q_g, k_g, rpb):
    b, l, d3 = qkv.shape
    d = d3 // 3
    nctx = qkvc.shape[1]
    hd = d // NA_HEADS
    rows = l // GRID_W
    blocks, patterns = _na_plan(rows)
    nq, nk = NA_QROWS * GRID_W, NA_KROWS * GRID_W
    seq = lambda col0: pl.BlockSpec((1, l, hd), lambda h, bi: (bi, 0, col0 + h))
    ctx = lambda col0: pl.BlockSpec((1, nctx, hd), lambda h, bi: (bi, 0, col0 + h))
    gain = pl.BlockSpec((1, hd), lambda h, bi: (0, 0))
    return pl.pallas_call(
        functools.partial(_na_kernel, blocks=blocks, patterns=patterns, hd=hd),
        grid=(NA_HEADS, b),
        in_specs=[pl.BlockSpec(memory_space=pltpu.SMEM),
                  seq(0), seq(NA_HEADS), seq(2 * NA_HEADS), ctx(NA_HEADS), ctx(2 * NA_HEADS), gain, gain],
        out_specs=pl.BlockSpec((1, l, hd), lambda h, bi: (bi, 0, h)),
        out_shape=jax.ShapeDtypeStruct((b, l, d), BF16),
        scratch_shapes=[pltpu.VMEM((2 * WIN_H - 1, GRID_W, GRID_W), F32),
                        pltpu.VMEM((len(patterns), nq, nk), F32),
                        pltpu.VMEM((l, hd), BF16), pltpu.VMEM((l, hd), BF16), pltpu.VMEM((nctx, hd), BF16)],
        compiler_params=_params("arbitrary", "arbitrary"),
        name="neighbourhood_attention",
    )(rpb.reshape(-1), qkv, qkv, qkv, qkvc, qkvc, q_g.reshape(1, hd), k_g.reshape(1, hd))


def kernel(x, c, ctx, c_ctx, mod_w, mod_b, norm1_g, norm2_g, a_w_pw1, a_b_pw1, a_w_dw, a_b_dw, a_ln_g, a_ln_b,
           a_w_pw2, a_b_pw2, b_w_out, b_b_out, c_w_qkv, c_q_g, c_k_g, c_rpb, c_w_o, f_w_up, f_w_dw, f_b_dw, f_w_down):
    bsz, seq, d = x.shape
    nctx = ctx.shape[1]
    depth = mod_w.shape[0]
    f = f_w_down.shape[1]
    n_mix = 3

    cond = jnp.zeros((MOD_ROWS, d), F32).at[:bsz].set(c).at[bsz].set(c_ctx)
    mods = _modulation(cond, mod_w, mod_b)

    def mod_vecs(i, first_row, n_rows):
        m = mods[i, first_row:first_row + n_rows]
        return [m[:, None, k * d:(k + 1) * d] for k in range(6)]

    ctx_out = [any(l % n_mix == 2 for l in range(i + 1, depth)) for i in range(depth)]
    ctx_in = [ctx_out[i] or i % n_mix == 2 for i in range(depth)]

    flat = lambda t: t.reshape(1, bsz * nctx, t.shape[-1])
    unflat = lambda t: t.reshape(bsz, nctx, t.shape[-1])

    wup_all = _prep_up(f_w_up)
    wdn_all = f_w_down.astype(BF16)

    mx, mc = mod_vecs(0, 0, bsz), mod_vecs(0, bsz, 1)
    h = _normmod(x, norm1_g[0], mx[0], mx[1])
    hc = _normmod(ctx, norm1_g[0], mc[0], mc[1]) if ctx_in[0] else None

    for i in range(depth):
        kind, j = i % n_mix, i // n_mix
        mx, mc = mod_vecs(i, 0, bsz), mod_vecs(i, bsz, 1)
        last = i + 1 == depth
        nmx, nmc = (None, None) if last else (mod_vecs(i + 1, 0, bsz), mod_vecs(i + 1, bsz, 1))

        if kind == 0:
            w1, w2 = a_w_pw1[j].astype(BF16), a_w_pw2[j].astype(BF16)

            def mixer(hh, res, gate, nxt, shared):
                fl, un = (flat, unflat) if shared else ((lambda t: t), (lambda t: t))
                u = un(_matmul_glu(fl(hh), w1, a_b_pw1[j]))
                u = _dwconv(u, a_w_dw[j], a_b_dw[j])
                xo, ho = _gated_matmul(fl(u), w2, a_b_pw2[j], fl(res), gate, ln=(a_ln_g[j], a_ln_b[j]), nxt=nxt)
                return un(xo), un(ho)
        elif kind == 1:
            wo = b_w_out[j].astype(BF16)

            def mixer(hh, res, gate, nxt, shared):
                fl, un = (flat, unflat) if shared else ((lambda t: t), (lambda t: t))
                xo, ho = _gated_matmul(fl(_fourier_mix(hh)), wo, b_b_out[j], fl(res), gate, nxt=nxt)
                return un(xo), un(ho)
        else:
            wqkv, wo = c_w_qkv[j].astype(BF16), c_w_o[j].astype(BF16)
            qkvc = unflat(_matmul(flat(hc), wqkv))

            def mixer(hh, res, gate, nxt, shared):
                assert not shared, "context outputs of the attention mixer are never consumed"
                o = _neighbourhood_attention(_matmul(hh, wqkv), qkvc, c_q_g[j], c_k_g[j], c_rpb[j])
                return _gated_matmul(o, wo, jnp.zeros((d,), F32), res, gate, nxt=nxt)

        wdw = _chunk_pairs(f_w_dw[i], f, FFN_SUB)
        bdw = _chunk_pairs(f_b_dw[i], f, FFN_SUB).reshape(1, 2 * f)

        x, h2 = mixer(h, x, mx[2], (norm2_g[i], mx[3], mx[4]), False)
        x, h = _conv_ffn(h2, x, i, wup_all, wdw, bdw, wdn_all, mx[5],
                         nxt=None if last else (norm1_g[i + 1], nmx[0], nmx[1]))
        if ctx_out[i]:
            ctx, hc2 = mixer(hc, ctx, mc[2], (norm2_g[i], mc[3], mc[4]), True)
            ctx, hc = _conv_ffn(hc2, ctx, i, wup_all, wdw, bdw, wdn_all, mc[5],
                                nxt=(norm1_g[i + 1], nmc[0], nmc[1]) if ctx_in[i + 1] else None)
    return x
```

```python
import functools
import math

import jax
import jax.numpy as jnp
from jax import lax
from jax.experimental import pallas as pl
from jax.experimental.pallas import tpu as pltpu

GRID_W = 64
FNET_GROUPS = 8
NA_HEADS = 16
WIN_H = 8
WIN_W = 16
EPS = 1e-6

F32 = jnp.float32
BF16 = jnp.bfloat16
MASKED = -1e30
V7X_VMEM_BYTES = 64 * 1024 * 1024
VMEM_LIMIT = V7X_VMEM_BYTES - 6 * 1024 * 1024
BF16_ROWS = 16
LANES = 128
MOD_ROWS = 32


def _params(*sem):
    return pltpu.CompilerParams(dimension_semantics=sem, vmem_limit_bytes=VMEM_LIMIT)


def _dot(a, b):
    return jnp.dot(a, b, preferred_element_type=F32)


def _silu(t):
    return t * jax.nn.sigmoid(t)


def _rms_mod(x, g, shift, scale):
    y = x * lax.rsqrt(jnp.mean(x * x, axis=-1, keepdims=True) + EPS) * g
    return y * (1.0 + scale) + shift


def _fit(n, pref, unit=LANES):
    t = min(pref, n) // unit * unit
    while n % t:
        t -= unit
    return t


def _bsel(n):
    return (lambda b: b) if n > 1 else (lambda b: 0)


def _mod_kernel(c_ref, w_ref, b_ref, o_ref):
    s = _silu(c_ref[...]).astype(BF16)
    o_ref[0] = _dot(s, w_ref[0].astype(BF16)) + b_ref[0]


def _modulation(cond, mod_w, mod_b, tn=1024):
    depth, d, n = mod_w.shape
    rows = cond.shape[0]
    tn = _fit(n, tn)
    return pl.pallas_call(
        _mod_kernel,
        grid=(depth, n // tn),
        in_specs=[pl.BlockSpec((rows, d), lambda l, j: (0, 0)),
                  pl.BlockSpec((1, d, tn), lambda l, j: (l, 0, j)),
                  pl.BlockSpec((1, 1, tn), lambda l, j: (l, 0, j))],
        out_specs=pl.BlockSpec((1, rows, tn), lambda l, j: (l, 0, j)),
        out_shape=jax.ShapeDtypeStruct((depth, rows, n), F32),
        compiler_params=_params("parallel", "parallel"),
        name="modulation",
    )(cond, mod_w, mod_b.reshape(depth, 1, n))


def _normmod_kernel(x_ref, g_ref, sh_ref, sc_ref, o_ref):
    o_ref[0] = _rms_mod(x_ref[0], g_ref[...], sh_ref[0], sc_ref[0]).astype(BF16)


def _normmod(x, g, shift, scale, tl=512):
    b, l, d = x.shape
    tl = min(tl, l)
    sel = _bsel(shift.shape[0])
    vec = pl.BlockSpec((1, 1, d), lambda bi, i: (sel(bi), 0, 0))
    return pl.pallas_call(
        _normmod_kernel,
        grid=(b, l // tl),
        in_specs=[pl.BlockSpec((1, tl, d), lambda bi, i: (bi, i, 0)),
                  pl.BlockSpec((1, d), lambda bi, i: (0, 0)), vec, vec],
        out_specs=pl.BlockSpec((1, tl, d), lambda bi, i: (bi, i, 0)),
        out_shape=jax.ShapeDtypeStruct((b, l, d), BF16),
        compiler_params=_params("parallel", "parallel"),
        name="normmod",
    )(x, g.reshape(1, d), shift, scale)


def _mm_kernel(a_ref, w_ref, o_ref):
    o_ref[0] = _dot(a_ref[0], w_ref[...]).astype(o_ref.dtype)


def _glu_kernel(a_ref, wa_ref, wg_ref, ba_ref, bg_ref, o_ref):
    a = a_ref[0]
    ua = _dot(a, wa_ref[...]) + ba_ref[...]
    ug = _dot(a, wg_ref[...]) + bg_ref[...]
    o_ref[0] = (ua * jax.nn.sigmoid(ug)).astype(o_ref.dtype)


def _matmul(a, w, tm=2048, tn=512):
    b, l, k = a.shape
    n = w.shape[1]
    tm, tn = min(tm, l), _fit(n, tn)
    return pl.pallas_call(
        _mm_kernel,
        grid=(b, l // tm, n // tn),
        in_specs=[pl.BlockSpec((1, tm, k), lambda bi, i, j: (bi, i, 0)),
                  pl.BlockSpec((k, tn), lambda bi, i, j: (0, j))],
        out_specs=pl.BlockSpec((1, tm, tn), lambda bi, i, j: (bi, i, j)),
        out_shape=jax.ShapeDtypeStruct((b, l, n), BF16),
        compiler_params=_params("parallel", "parallel", "arbitrary"),
        name="matmul",
    )(a, w)


def _matmul_glu(a, w, bias, tm=2048, tn=512):
    b, l, k = a.shape
    n = w.shape[1] // 2
    tm, tn = min(tm, l), _fit(n, tn)
    nj = n // tn
    bias = bias.reshape(1, 2 * n)
    return pl.pallas_call(
        _glu_kernel,
        grid=(b, l // tm, nj),
        in_specs=[pl.BlockSpec((1, tm, k), lambda bi, i, j: (bi, i, 0)),
                  pl.BlockSpec((k, tn), lambda bi, i, j: (0, j)),
                  pl.BlockSpec((k, tn), lambda bi, i, j: (0, j + nj)),
                  pl.BlockSpec((1, tn), lambda bi, i, j: (0, j)),
                  pl.BlockSpec((1, tn), lambda bi, i, j: (0, j + nj))],
        out_specs=pl.BlockSpec((1, tm, tn), lambda bi, i, j: (bi, i, j)),
        out_shape=jax.ShapeDtypeStruct((b, l, n), BF16),
        compiler_params=_params("parallel", "parallel", "arbitrary"),
        name="matmul_glu",
    )(a, w, w, bias, bias)


GATED_SUB = 256


def _gated_kernel(*refs, ln, emit_h, nsub):
    it = iter(refs)
    a_ref, w_ref, bias_ref, res_ref, gate_ref = (next(it) for _ in range(5))
    if ln:
        lng_ref, lnb_ref = next(it), next(it)
    if emit_h:
        ng_ref, nsh_ref, nsc_ref = next(it), next(it), next(it)
    o_ref = next(it)
    h_ref = next(it) if emit_h else None
    sub = a_ref.shape[1] // nsub

    def prologue(r):
        a = a_ref[0, r * sub:(r + 1) * sub, :]
        if not ln:
            return a
        t = a.astype(F32)
        mu = jnp.mean(t, axis=-1, keepdims=True)
        var = jnp.mean(jnp.square(t - mu), axis=-1, keepdims=True)
        t = (t - mu) * lax.rsqrt(var + EPS) * lng_ref[...] + lnb_ref[...]
        return _silu(t).astype(BF16)

    lhs = [prologue(r) for r in range(nsub)]
    for r in range(nsub):
        rows = slice(r * sub, (r + 1) * sub)
        xn = res_ref[0, rows, :] + gate_ref[0] * (_dot(lhs[r], w_ref[...]) + bias_ref[...])
        o_ref[0, rows, :] = xn
        if emit_h:
            h_ref[0, rows, :] = _rms_mod(xn, ng_ref[...], nsh_ref[0], nsc_ref[0]).astype(BF16)


def _gated_matmul(a, w, bias, res, gate, ln=None, nxt=None, tm=512):
    b, l, k = a.shape
    d = w.shape[1]
    tm = min(tm, l)
    nsub = max(tm // GATED_SUB, 1)
    row = lambda bi, i: (bi, i, 0)
    const = lambda bi, i: (0, 0)

    def vec(arr):
        sel = _bsel(arr.shape[0])
        return pl.BlockSpec((1, 1, d), lambda bi, i: (sel(bi), 0, 0))

    args = [a, w, bias.reshape(1, d), res, gate]
    specs = [pl.BlockSpec((1, tm, k), row),
             pl.BlockSpec((k, d), const, pipeline_mode=pl.Buffered(1)),
             pl.BlockSpec((1, d), const), pl.BlockSpec((1, tm, d), row), vec(gate)]
    if ln is not None:
        args += [ln[0].reshape(1, k), ln[1].reshape(1, k)]
        specs += [pl.BlockSpec((1, k), const)] * 2
    out_shape = [jax.ShapeDtypeStruct((b, l, d), F32)]
    out_specs = [pl.BlockSpec((1, tm, d), row)]
    if nxt is not None:
        args += [nxt[0].reshape(1, d), nxt[1], nxt[2]]
        specs += [pl.BlockSpec((1, d), const), vec(nxt[1]), vec(nxt[2])]
        out_shape.append(jax.ShapeDtypeStruct((b, l, d), BF16))
        out_specs.append(pl.BlockSpec((1, tm, d), row))
    out = pl.pallas_call(
        functools.partial(_gated_kernel, ln=ln is not None, emit_h=nxt is not None, nsub=nsub),
        grid=(b, l // tm), in_specs=specs, out_specs=out_specs, out_shape=out_shape,
        compiler_params=_params("parallel", "parallel"),
        name="gated_matmul",
    )(*args)
    return (out[0], out[1]) if nxt is not None else (out[0], None)


def _dwconv_kernel(x_ref, w_ref, b_ref, o_ref, pad_ref, *, width, rt):
    l, tc = x_ref.shape[1], x_ref.shape[2]
    halo = BF16_ROWS
    first = halo - width // 2
    pad_ref[0:halo, :] = jnp.zeros((halo, tc), F32)
    pad_ref[halo + l:halo + l + halo, :] = jnp.zeros((halo, tc), F32)
    pad_ref[halo:halo + l, :] = x_ref[0].astype(F32)

    def body(t, carry):
        base = pl.multiple_of(t * rt, rt)
        acc = jnp.zeros((rt, tc), F32) + b_ref[...]
        for k in range(width):
            acc = acc + pad_ref[pl.ds(base + first + k, rt), :] * w_ref[k:k + 1, :]
        o_ref[0, pl.ds(base, rt), :] = acc.astype(o_ref.dtype)
        return carry

    lax.fori_loop(0, l // rt, body, 0)


def _dwconv(x, w, bias, rt=128):
    b, l, c = x.shape
    width = w.shape[0]
    tc = LANES
    assert width // 2 <= BF16_ROWS
    return pl.pallas_call(
        functools.partial(_dwconv_kernel, width=width, rt=rt),
        grid=(b, c // tc),
        in_specs=[pl.BlockSpec((1, l, tc), lambda bi, j: (bi, 0, j)),
                  pl.BlockSpec((width, tc), lambda bi, j: (0, j)),
                  pl.BlockSpec((1, tc), lambda bi, j: (0, j))],
        out_specs=pl.BlockSpec((1, l, tc), lambda bi, j: (bi, 0, j)),
        out_shape=jax.ShapeDtypeStruct((b, l, c), BF16),
        scratch_shapes=[pltpu.VMEM((l + 2 * BF16_ROWS, tc), F32)],
        compiler_params=_params("parallel", "parallel"),
        name="dwconv",
    )(x, w, bias.reshape(1, c))


FFN_SUB = 256
FFN_NSUB = 2
FFN_ROWS = 1024


def _prep_up_kernel(v_ref, g_ref, o_ref, *, fs, ns):
    for s in range(ns):
        o_ref[0, 0, :, 2 * fs * s:2 * fs * s + fs] = v_ref[0, :, fs * s:fs * (s + 1)].astype(BF16)
        o_ref[0, 0, :, 2 * fs * s + fs:2 * fs * (s + 1)] = g_ref[0, :, fs * s:fs * (s + 1)].astype(BF16)


def _prep_up(w):
    depth, d, f2 = w.shape
    fs, ns = FFN_SUB, FFN_NSUB
    fc = fs * ns
    nc = f2 // 2 // fc
    return pl.pallas_call(
        functools.partial(_prep_up_kernel, fs=fs, ns=ns),
        grid=(depth, nc),
        in_specs=[pl.BlockSpec((1, d, fc), lambda l, c: (l, 0, c)),
                  pl.BlockSpec((1, d, fc), lambda l, c: (l, 0, c + nc))],
        out_specs=pl.BlockSpec((1, 1, d, 2 * fc), lambda l, c: (l, c, 0, 0)),
        out_shape=jax.ShapeDtypeStruct((depth, nc, d, 2 * fc), BF16),
        compiler_params=_params("parallel", "parallel"),
        name="prep_up",
    )(w, w)


def _chunk_pairs(t, f, fs):
    lead = t.shape[:-1]
    t = t.reshape(lead + (2, f // fs, fs))
    return jnp.swapaxes(t, -3, -2).reshape(lead + (2 * f,))


def _ffn_kernel(*refs, nt, nseg, seg, halo, fs, ns, nc, rc, emit_h):
    it = iter(refs)
    if nt > 1:
        hprev_ref, hnext_ref = next(it), next(it)
    h_hbm, x_hbm, wup_ref, wdw_ref, bdw_ref, wdn_ref, gate_ref = (next(it) for _ in range(7))
    if emit_h:
        ng_ref, nsh_ref, nsc_ref = next(it), next(it), next(it)
    o_ref = next(it)
    hn_ref = next(it) if emit_h else None
    hp_ref, u_ref, x_sem, h_sem = it
    bi = pl.program_id(0)
    i = pl.program_id(1)
    c = pl.program_id(2)
    d = hp_ref.shape[1]
    rows = nseg * seg
    stride = seg + halo
    lt = 2 * fs // LANES

    def x_copy():
        return pltpu.make_async_copy(x_hbm.at[bi, pl.ds(i * rows, rows), :], o_ref.at[0], x_sem)

    def h_copy(k):
        return pltpu.make_async_copy(h_hbm.at[bi, pl.ds(i * rows + k * seg, seg), :],
                                     hp_ref.at[pl.ds(halo + k * stride, seg), :], h_sem.at[k])

    @pl.when(c == 0)
    def _():
        x_copy().start()
        for k in range(nseg):
            h_copy(k).start()
        zero = jnp.zeros((halo, d), BF16)
        for k in range(nseg + 1):
            band = zero
            if nt > 1 and k == 0:
                band = jnp.where(i > 0, hprev_ref[0], zero)
            if nt > 1 and k == nseg:
                band = jnp.where(i < nt - 1, hnext_ref[0], zero)
            hp_ref[k * stride:k * stride + halo, :] = band
        for k in range(nseg):
            h_copy(k).wait()
        x_copy().wait()

    hp = hp_ref[...]
    for s in range(ns):
        u = _dot(hp, wup_ref[0, 0, :, 2 * fs * s:2 * fs * (s + 1)])
        for t in range(lt):
            u_ref[s * lt + t] = u[:, LANES * t:LANES * (t + 1)]
    for s in range(ns):
        def conv(t, k):
            col = 2 * fs * s + LANES * t
            w = wdw_ref[:, col:col + LANES]
            r0 = halo + k * stride
            ut = u_ref.at[s * lt + t]
            return (ut[r0 - 1:r0 - 1 + seg, :] * w[0:1] + ut[r0:r0 + seg, :] * w[1:2]
                    + ut[r0 + 1:r0 + 1 + seg, :] * w[2:3] + bdw_ref[:, col:col + LANES])

        act = jnp.concatenate(
            [jnp.concatenate([(_silu(conv(lt // 2 + t, k)) * conv(t, k)).astype(BF16) for t in range(lt // 2)],
                             axis=1) for k in range(nseg)], axis=0)
        o_ref[0] += gate_ref[0] * _dot(act, wdn_ref[0, fs * s:fs * (s + 1), :])

    if emit_h:
        @pl.when(c == nc - 1)
        def _():
            def rows_body(t, carry):
                r = pl.multiple_of(t * rc, rc)
                xn = o_ref[0, pl.ds(r, rc), :]
                hn_ref[0, pl.ds(r, rc), :] = _rms_mod(xn, ng_ref[...], nsh_ref[0], nsc_ref[0]).astype(BF16)
                return carry

            lax.fori_loop(0, rows // rc, rows_body, 0)


def _conv_ffn(h, x, layer, wup, wdw, bdw, wdn, gate, nxt=None, rc=128):
    b, l, d = h.shape
    _, nc, _, cw = wup.shape
    fs = FFN_SUB
    ns = cw // (2 * fs)
    halo = BF16_ROWS
    if l >= FFN_ROWS:
        nseg, seg, nt, nb = 1, FFN_ROWS, l // FFN_ROWS, b
    else:
        groupable = gate.shape[0] == 1
        nseg = max(n for n in range(1, b + 1) if b % n == 0 and (n == 1 or (groupable and n * l <= FFN_ROWS)))
        seg, nt, nb = l, 1, b // nseg
    rows = nseg * seg
    m = nseg * (seg + halo) + halo
    tile = lambda bi, i, c: (bi, i, 0)
    args, specs = [], []
    if nt > 1:
        hb, nhb = rows // halo, l // halo
        args += [h, h]
        specs += [pl.BlockSpec((1, halo, d), lambda bi, i, c: (bi, jnp.maximum(i * hb - 1, 0), 0)),
                  pl.BlockSpec((1, halo, d), lambda bi, i, c: (bi, jnp.minimum((i + 1) * hb, nhb - 1), 0))]
    vec = lambda arr: pl.BlockSpec((1, 1, d), lambda bi, i, c: (_bsel(arr.shape[0])(bi), 0, 0))
    args += [h.reshape(nb, nt * rows, d), x.reshape(nb, nt * rows, d), wup, wdw, bdw, wdn, gate]
    specs += [pl.BlockSpec(memory_space=pl.ANY),
              pl.BlockSpec(memory_space=pl.ANY),
              pl.BlockSpec((1, 1, d, cw), lambda bi, i, c: (layer, c, 0, 0)),
              pl.BlockSpec((3, cw), lambda bi, i, c: (0, c)),
              pl.BlockSpec((1, cw), lambda bi, i, c: (0, c)),
              pl.BlockSpec((1, fs * ns, d), lambda bi, i, c: (layer, c, 0)),
              vec(gate)]
    out_shape = [jax.ShapeDtypeStruct((nb, nt * rows, d), F32)]
    out_specs = [pl.BlockSpec((1, rows, d), tile)]
    if nxt is not None:
        args += [nxt[0].reshape(1, d), nxt[1], nxt[2]]
        specs += [pl.BlockSpec((1, d), lambda bi, i, c: (0, 0)), vec(nxt[1]), vec(nxt[2])]
        out_shape.append(jax.ShapeDtypeStruct((nb, nt * rows, d), BF16))
        out_specs.append(pl.BlockSpec((1, rows, d), tile))
    out = pl.pallas_call(
        functools.partial(_ffn_kernel, nt=nt, nseg=nseg, seg=seg, halo=halo, fs=fs, ns=ns, nc=nc, rc=min(rc, rows),
                          emit_h=nxt is not None),
        grid=(nb, nt, nc), in_specs=specs, out_specs=out_specs, out_shape=out_shape,
        scratch_shapes=[pltpu.VMEM((m, d), BF16), pltpu.VMEM((ns * 2 * fs // LANES, m, LANES), F32),
                        pltpu.SemaphoreType.DMA(()), pltpu.SemaphoreType.DMA((nseg,))],
        compiler_params=_params("parallel", "parallel", "arbitrary"),
        name="conv_ffn",
    )(*args)
    return out[0].reshape(b, l, d), (out[1].reshape(b, l, d) if nxt is not None else None)


def _fnet_kernel(h_ref, cc_ref, sc_ref, csl_ref, o_ref, pq_ref, *, groups, gw):
    l = h_ref.shape[1]

    @pl.when(pl.program_id(1) == 0)
    def _():
        for g in range(groups):
            hg = h_ref[0, :, g * gw:(g + 1) * gw]
            pq_ref[0:l, g * gw:(g + 1) * gw] = _dot(hg, cc_ref[...]).astype(BF16)
            pq_ref[l:2 * l, g * gw:(g + 1) * gw] = _dot(hg, sc_ref[...]).astype(BF16)

    o_ref[0] = _dot(csl_ref[...], pq_ref[...]).astype(o_ref.dtype)


def _dft_tables(n):
    idx = jnp.arange(n, dtype=jnp.int32)
    ang = ((idx[:, None] * idx[None, :]) % n).astype(F32) * (2.0 * math.pi / n)
    s = 1.0 / math.sqrt(n)
    return jnp.cos(ang) * s, jnp.sin(ang) * s


def _fourier_mix(h, tl=512):
    b, l, d = h.shape
    gw = d // FNET_GROUPS
    tl = min(tl, l)
    cc, sc = _dft_tables(gw)
    cl, sl = _dft_tables(l)
    csl = jnp.concatenate([cl, -sl], axis=1).astype(BF16)
    return pl.pallas_call(
        functools.partial(_fnet_kernel, groups=FNET_GROUPS, gw=gw),
        grid=(b, l // tl),
        in_specs=[pl.BlockSpec((1, l, d), lambda bi, i: (bi, 0, 0)),
                  pl.BlockSpec((gw, gw), lambda bi, i: (0, 0)),
                  pl.BlockSpec((gw, gw), lambda bi, i: (0, 0)),
                  pl.BlockSpec((tl, 2 * l), lambda bi, i: (i, 0))],
        out_specs=pl.BlockSpec((1, tl, d), lambda bi, i: (bi, i, 0)),
        out_shape=jax.ShapeDtypeStruct((b, l, d), BF16),
        scratch_shapes=[pltpu.VMEM((2 * l, d), BF16)],
        compiler_params=_params("parallel", "arbitrary"),
        name="fourier_mix",
    )(h, cc.astype(BF16), sc.astype(BF16), csl)


NA_QROWS = 4
NA_KROWS = 12


def _na_plan(rows):
    assert rows % NA_QROWS == 0 and rows >= NA_KROWS
    patterns, blocks = [], []
    for j in range(rows // NA_QROWS):
        r0 = j * NA_QROWS
        ks = min(max(r0 - WIN_H // 2, 0), rows - NA_KROWS)
        pat = []
        for ql in range(NA_QROWS):
            r = r0 + ql
            rs = min(max(r - WIN_H // 2, 0), rows - WIN_H)
            assert ks <= rs and rs + WIN_H <= ks + NA_KROWS
            pat.append((rs - ks, ks - r + WIN_H - 1))
        pat = tuple(pat)
        if pat not in patterns:
            patterns.append(pat)
        blocks.append((ks, patterns.index(pat)))
    return tuple(blocks), tuple(patterns)


def _na_kernel(rpb_ref, q_ref, k_ref, v_ref, kc_ref, vc_ref, qg_ref, kg_ref, o_ref,
               base_ref, tab_ref, qn_ref, kn_ref, kcn_ref, *, blocks, patterns, hd):
    w = GRID_W
    ndr, ndc = 2 * WIN_H - 1, 2 * WIN_W - 1
    head = pl.program_id(0)

    @pl.when(pl.program_id(1) == 0)
    def _build_bias_tables():
        qi = lax.broadcasted_iota(jnp.int32, (w, w), 0)
        ki = lax.broadcasted_iota(jnp.int32, (w, w), 1)
        cstart = jnp.clip(qi - WIN_W // 2, 0, w - WIN_W)
        in_window = (ki >= cstart) & (ki < cstart + WIN_W)
        dci = jnp.clip(ki - qi, -(WIN_W - 1), WIN_W - 1) + (WIN_W - 1)
        for dr in range(ndr):
            t = jnp.zeros((w, w), F32)
            for dc in range(ndc):
                t = jnp.where(dci == dc, rpb_ref[(head * ndr + dr) * ndc + dc], t)
            base_ref[dr] = jnp.where(in_window, t, MASKED)
        for p, pat in enumerate(patterns):
            for ql, (first, droff) in enumerate(pat):
                for kl in range(NA_KROWS):
                    if first <= kl < first + WIN_H:
                        piece = base_ref[kl + droff]
                    else:
                        piece = jnp.full((w, w), MASKED, F32)
                    tab_ref[p, ql * w:(ql + 1) * w, kl * w:(kl + 1) * w] = piece

    def rms(t, g):
        t = t.astype(F32)
        return t * lax.rsqrt(jnp.mean(t * t, axis=-1, keepdims=True) + EPS) * g

    qn_ref[...] = (rms(q_ref[0], qg_ref[...]) * (hd ** -0.5)).astype(BF16)
    kn_ref[...] = rms(k_ref[0], kg_ref[...]).astype(BF16)
    kcn_ref[...] = rms(kc_ref[0], kg_ref[...]).astype(BF16)

    nt = (((1,), (1,)), ((), ()))
    nq, nk = NA_QROWS * w, NA_KROWS * w
    for j, (ks, p) in enumerate(blocks):
        qb = qn_ref[j * nq:(j + 1) * nq, :]
        s1 = lax.dot_general(qb, kn_ref[ks * w:ks * w + nk, :], nt, preferred_element_type=F32) + tab_ref[p]
        s2 = lax.dot_general(qb, kcn_ref[...], nt, preferred_element_type=F32)
        mx = jnp.maximum(jnp.max(s1, axis=-1, keepdims=True), jnp.max(s2, axis=-1, keepdims=True))
        p1 = jnp.exp(s1 - mx)
        p2 = jnp.exp(s2 - mx)
        den = jnp.sum(p1, axis=-1, keepdims=True) + jnp.sum(p2, axis=-1, keepdims=True)
        o = _dot(p1.astype(BF16), v_ref[0, ks * w:ks * w + nk, :]) + _dot(p2.astype(BF16), vc_ref[0])
        o_ref[0, j * nq:(j + 1) * nq, :] = (o / den).astype(o_ref.dtype)


def _neighbourhood_attention(qkv, qkvc, q_g, k_g, rpb):
    b, l, d3 = qkv.shape
    d = d3 // 3
    nctx = qkvc.shape[1]
    hd = d // NA_HEADS
    rows = l // GRID_W
    blocks, patterns = _na_plan(rows)
    nq, nk = NA_QROWS * GRID_W, NA_KROWS * GRID_W
    seq = lambda col0: pl.BlockSpec((1, l, hd), lambda h, bi: (bi, 0, col0 + h))
    ctx = lambda col0: pl.BlockSpec((1, nctx, hd), lambda h, bi: (bi, 0, col0 + h))
    gain = pl.BlockSpec((1, hd), lambda h, bi: (0, 0))
    return pl.pallas_call(
        functools.partial(_na_kernel, blocks=blocks, patterns=patterns, hd=hd),
        grid=(NA_HEADS, b),
        in_specs=[pl.BlockSpec(memory_space=pltpu.SMEM),
                  seq(0), seq(NA_HEADS), seq(2 * NA_HEADS), ctx(NA_HEADS), ctx(2 * NA_HEADS), gain, gain],
        out_specs=pl.BlockSpec((1, l, hd), lambda h, bi: (bi, 0, h)),
        out_shape=jax.ShapeDtypeStruct((b, l, d), BF16),
        scratch_shapes=[pltpu.VMEM((2 * WIN_H - 1, GRID_W, GRID_W), F32),
                        pltpu.VMEM((len(patterns), nq, nk), F32),
                        pltpu.VMEM((l, hd), BF16), pltpu.VMEM((l, hd), BF16), pltpu.VMEM((nctx, hd), BF16)],
        compiler_params=_params("arbitrary", "arbitrary"),
        name="neighbourhood_attention",
    )(rpb.reshape(-1), qkv, qkv, qkv, qkvc, qkvc, q_g.reshape(1, hd), k_g.reshape(1, hd))


def kernel(x, c, ctx, c_ctx, mod_w, mod_b, norm1_g, norm2_g, a_w_pw1, a_b_pw1, a_w_dw, a_b_dw, a_ln_g, a_ln_b,
           a_w_pw2, a_b_pw2, b_w_out, b_b_out, c_w_qkv, c_q_g, c_k_g, c_rpb, c_w_o, f_w_up, f_w_dw, f_b_dw, f_w_down):
    bsz, seq, d = x.shape
    nctx = ctx.shape[1]
    depth = mod_w.shape[0]
    f = f_w_down.shape[1]
    n_mix = 3

    cond = jnp.zeros((MOD_ROWS, d), F32).at[:bsz].set(c).at[bsz].set(c_ctx)
    mods = _modulation(cond, mod_w, mod_b)

    def mod_vecs(i, first_row, n_rows):
        m = mods[i, first_row:first_row + n_rows]
        return [m[:, None, k * d:(k + 1) * d] for k in range(6)]

    ctx_out = [any(l % n_mix == 2 for l in range(i + 1, depth)) for i in range(depth)]
    ctx_in = [ctx_out[i] or i % n_mix == 2 for i in range(depth)]

    flat = lambda t: t.reshape(1, bsz * nctx, t.shape[-1])
    unflat = lambda t: t.reshape(bsz, nctx, t.shape[-1])

    wup_all = _prep_up(f_w_up)
    wdn_all = f_w_down.astype(BF16)

    mx, mc = mod_vecs(0, 0, bsz), mod_vecs(0, bsz, 1)
    h = _normmod(x, norm1_g[0], mx[0], mx[1])
    hc = _normmod(ctx, norm1_g[0], mc[0], mc[1]) if ctx_in[0] else None

    for i in range(depth):
        kind, j = i % n_mix, i // n_mix
        mx, mc = mod_vecs(i, 0, bsz), mod_vecs(i, bsz, 1)
        last = i + 1 == depth
        nmx, nmc = (None, None) if last else (mod_vecs(i + 1, 0, bsz), mod_vecs(i + 1, bsz, 1))

        if kind == 0:
            w1, w2 = a_w_pw1[j].astype(BF16), a_w_pw2[j].astype(BF16)

            def mixer(hh, res, gate, nxt, shared):
                fl, un = (flat, unflat) if shared else ((lambda t: t), (lambda t: t))
                u = un(_matmul_glu(fl(hh), w1, a_b_pw1[j]))
                u = _dwconv(u, a_w_dw[j], a_b_dw[j])
                xo, ho = _gated_matmul(fl(u), w2, a_b_pw2[j], fl(res), gate, ln=(a_ln_g[j], a_ln_b[j]), nxt=nxt)
                return un(xo), un(ho)
        elif kind == 1:
            wo = b_w_out[j].astype(BF16)

            def mixer(hh, res, gate, nxt, shared):
                fl, un = (flat, unflat) if shared else ((lambda t: t), (lambda t: t))
                xo, ho = _gated_matmul(fl(_fourier_mix(hh)), wo, b_b_out[j], fl(res), gate, nxt=nxt)
                return un(xo), un(ho)
        else:
            wqkv, wo = c_w_qkv[j].astype(BF16), c_w_o[j].astype(BF16)
            qkvc = unflat(_matmul(flat(hc), wqkv))

            def mixer(hh, res, gate, nxt, shared):
                assert not shared, "context outputs of the attention mixer are never consumed"
                o = _neighbourhood_attention(_matmul(hh, wqkv), qkvc, c_q_g[j], c_k_g[j], c_rpb[j])
                return _gated_matmul(o, wo, jnp.zeros((d,), F32), res, gate, nxt=nxt)

        wdw = _chunk_pairs(f_w_dw[i], f, FFN_SUB)
        bdw = _chunk_pairs(f_b_dw[i], f, FFN_SUB).reshape(1, 2 * f)

        x, h2 = mixer(h, x, mx[2], (norm2_g[i], mx[3], mx[4]), False)
        x, h = _conv_ffn(h2, x, i, wup_all, wdw, bdw, wdn_all, mx[5],
                         nxt=None if last else (norm1_g[i + 1], nmx[0], nmx[1]))
        if ctx_out[i]:
            ctx, hc2 = mixer(hc, ctx, mc[2], (norm2_g[i], mc[3], mc[4]), True)
            ctx, hc = _conv_ffn(hc2, ctx, i, wup_all, wdw, bdw, wdn_all, mc[5],
                                nxt=(norm1_g[i + 1], nmc[0], nmc[1]) if ctx_in[i + 1] else None)
    return x
```

```python
import functools
import math

import jax
import jax.numpy as jnp
from jax import lax
from jax.experimental import pallas as pl
from jax.experimental.pallas import tpu as pltpu

GRID_W = 64
FNET_GROUPS = 8
NA_HEADS = 16
WIN_H = 8
WIN_W = 16
EPS = 1e-6

F32 = jnp.float32
BF16 = jnp.bfloat16
MASKED = -1e30
V7X_VMEM_BYTES = 64 * 1024 * 1024
VMEM_LIMIT = V7X_VMEM_BYTES - 6 * 1024 * 1024
BF16_ROWS = 16
LANES = 128
MOD_ROWS = 32


def _params(*sem):
    return pltpu.CompilerParams(dimension_semantics=sem, vmem_limit_bytes=VMEM_LIMIT)


def _dot(a, b):
    return jnp.dot(a, b, preferred_element_type=F32)


def _silu(t):
    return t * jax.nn.sigmoid(t)


def _rms_mod(x, g, shift, scale):
    y = x * lax.rsqrt(jnp.mean(x * x, axis=-1, keepdims=True) + EPS) * g
    return y * (1.0 + scale) + shift


def _fit(n, pref, unit=LANES):
    t = min(pref, n) // unit * unit
    while n % t:
        t -= unit
    return t


def _bsel(n):
    return (lambda b: b) if n > 1 else (lambda b: 0)


def _mod_kernel(c_ref, w_ref, b_ref, o_ref):
    s = _silu(c_ref[...]).astype(BF16)
    o_ref[0] = _dot(s, w_ref[0].astype(BF16)) + b_ref[0]


def _modulation(cond, mod_w, mod_b, tn=1024):
    depth, d, n = mod_w.shape
    rows = cond.shape[0]
    tn = _fit(n, tn)
    return pl.pallas_call(
        _mod_kernel,
        grid=(depth, n // tn),
        in_specs=[pl.BlockSpec((rows, d), lambda l, j: (0, 0)),
                  pl.BlockSpec((1, d, tn), lambda l, j: (l, 0, j)),
                  pl.BlockSpec((1, 1, tn), lambda l, j: (l, 0, j))],
        out_specs=pl.BlockSpec((1, rows, tn), lambda l, j: (l, 0, j)),
        out_shape=jax.ShapeDtypeStruct((depth, rows, n), F32),
        compiler_params=_params("parallel", "parallel"),
        name="modulation",
    )(cond, mod_w, mod_b.reshape(depth, 1, n))


def _normmod_kernel(x_ref, g_ref, sh_ref, sc_ref, o_ref):
    o_ref[0] = _rms_mod(x_ref[0], g_ref[...], sh_ref[0], sc_ref[0]).astype(BF16)


def _normmod(x, g, shift, scale, tl=512):
    b, l, d = x.shape
    tl = min(tl, l)
    sel = _bsel(shift.shape[0])
    vec = pl.BlockSpec((1, 1, d), lambda bi, i: (sel(bi), 0, 0))
    return pl.pallas_call(
        _normmod_kernel,
        grid=(b, l // tl),
        in_specs=[pl.BlockSpec((1, tl, d), lambda bi, i: (bi, i, 0)),
                  pl.BlockSpec((1, d), lambda bi, i: (0, 0)), vec, vec],
        out_specs=pl.BlockSpec((1, tl, d), lambda bi, i: (bi, i, 0)),
        out_shape=jax.ShapeDtypeStruct((b, l, d), BF16),
        compiler_params=_params("parallel", "parallel"),
        name="normmod",
    )(x, g.reshape(1, d), shift, scale)


def _mm_kernel(a_ref, w_ref, o_ref):
    o_ref[0] = _dot(a_ref[0], w_ref[...]).astype(o_ref.dtype)


def _glu_kernel(a_ref, wa_ref, wg_ref, ba_ref, bg_ref, o_ref):
    a = a_ref[0]
    ua = _dot(a, wa_ref[...]) + ba_ref[...]
    ug = _dot(a, wg_ref[...]) + bg_ref[...]
    o_ref[0] = (ua * jax.nn.sigmoid(ug)).astype(o_ref.dtype)


def _matmul(a, w, tm=2048, tn=512):
    b, l, k = a.shape
    n = w.shape[1]
    tm, tn = min(tm, l), _fit(n, tn)
    return pl.pallas_call(
        _mm_kernel,
        grid=(b, l // tm, n // tn),
        in_specs=[pl.BlockSpec((1, tm, k), lambda bi, i, j: (bi, i, 0)),
                  pl.BlockSpec((k, tn), lambda bi, i, j: (0, j))],
        out_specs=pl.BlockSpec((1, tm, tn), lambda bi, i, j: (bi, i, j)),
        out_shape=jax.ShapeDtypeStruct((b, l, n), BF16),
        compiler_params=_params("parallel", "parallel", "arbitrary"),
        name="matmul",
    )(a, w)


def _matmul_glu(a, w, bias, tm=2048, tn=512):
    b, l, k = a.shape
    n = w.shape[1] // 2
    tm, tn = min(tm, l), _fit(n, tn)
    nj = n // tn
    bias = bias.reshape(1, 2 * n)
    return pl.pallas_call(
        _glu_kernel,
        grid=(b, l // tm, nj),
        in_specs=[pl.BlockSpec((1, tm, k), lambda bi, i, j: (bi, i, 0)),
                  pl.BlockSpec((k, tn), lambda bi, i, j: (0, j)),
                  pl.BlockSpec((k, tn), lambda bi, i, j: (0, j + nj)),
                  pl.BlockSpec((1, tn), lambda bi, i, j: (0, j)),
                  pl.BlockSpec((1, tn), lambda bi, i, j: (0, j + nj))],
        out_specs=pl.BlockSpec((1, tm, tn), lambda bi, i, j: (bi, i, j)),
        out_shape=jax.ShapeDtypeStruct((b, l, n), BF16),
        compiler_params=_params("parallel", "parallel", "arbitrary"),
        name="matmul_glu",
    )(a, w, w, bias, bias)


GATED_SUB = 256


def _gated_kernel(*refs, ln, emit_h, nsub):
    it = iter(refs)
    a_ref, w_ref, bias_ref, res_ref, gate_ref = (next(it) for _ in range(5))
    if ln:
        lng_ref, lnb_ref = next(it), next(it)
    if emit_h:
        ng_ref, nsh_ref, nsc_ref = next(it), next(it), next(it)
    o_ref = next(it)
    h_ref = next(it) if emit_h else None
    sub = a_ref.shape[1] // nsub

    def prologue(r):
        a = a_ref[0, r * sub:(r + 1) * sub, :]
        if not ln:
            return a
        t = a.astype(F32)
        mu = jnp.mean(t, axis=-1, keepdims=True)
        var = jnp.mean(jnp.square(t - mu), axis=-1, keepdims=True)
        t = (t - mu) * lax.rsqrt(var + EPS) * lng_ref[...] + lnb_ref[...]
        return _silu(t).astype(BF16)

    lhs = [prologue(r) for r in range(nsub)]
    for r in range(nsub):
        rows = slice(r * sub, (r + 1) * sub)
        xn = res_ref[0, rows, :] + gate_ref[0] * (_dot(lhs[r], w_ref[...]) + bias_ref[...])
        o_ref[0, rows, :] = xn
        if emit_h:
            h_ref[0, rows, :] = _rms_mod(xn, ng_ref[...], nsh_ref[0], nsc_ref[0]).astype(BF16)


def _gated_matmul(a, w, bias, res, gate, ln=None, nxt=None, tm=512):
    b, l, k = a.shape
    d = w.shape[1]
    tm = min(tm, l)
    nsub = max(tm // GATED_SUB, 1)
    row = lambda bi, i: (bi, i, 0)
    const = lambda bi, i: (0, 0)

    def vec(arr):
        sel = _bsel(arr.shape[0])
        return pl.BlockSpec((1, 1, d), lambda bi, i: (sel(bi), 0, 0))

    args = [a, w, bias.reshape(1, d), res, gate]
    specs = [pl.BlockSpec((1, tm, k), row),
             pl.BlockSpec((k, d), const, pipeline_mode=pl.Buffered(1)),
             pl.BlockSpec((1, d), const), pl.BlockSpec((1, tm, d), row), vec(gate)]
    if ln is not None:
        args += [ln[0].reshape(1, k), ln[1].reshape(1, k)]
        specs += [pl.BlockSpec((1, k), const)] * 2
    out_shape = [jax.ShapeDtypeStruct((b, l, d), F32)]
    out_specs = [pl.BlockSpec((1, tm, d), row)]
    if nxt is not None:
        args += [nxt[0].reshape(1, d), nxt[1], nxt[2]]
        specs += [pl.BlockSpec((1, d), const), vec(nxt[1]), vec(nxt[2])]
        out_shape.append(jax.ShapeDtypeStruct((b, l, d), BF16))
        out_specs.append(pl.BlockSpec((1, tm, d), row))
    out = pl.pallas_call(
        functools.partial(_gated_kernel, ln=ln is not None, emit_h=nxt is not None, nsub=nsub),
        grid=(b, l // tm), in_specs=specs, out_specs=out_specs, out_shape=out_shape,
        compiler_params=_params("parallel", "parallel"),
        name="gated_matmul",
    )(*args)
    return (out[0], out[1]) if nxt is not None else (out[0], None)


def _dwconv_kernel(x_ref, w_ref, b_ref, o_ref, pad_ref, *, width, rt):
    l, tc = x_ref.shape[1], x_ref.shape[2]
    halo = BF16_ROWS
    first = halo - width // 2
    pad_ref[0:halo, :] = jnp.zeros((halo, tc), F32)
    pad_ref[halo + l:halo + l + halo, :] = jnp.zeros((halo, tc), F32)
    pad_ref[halo:halo + l, :] = x_ref[0].astype(F32)

    def body(t, carry):
        base = pl.multiple_of(t * rt, rt)
        acc = jnp.zeros((rt, tc), F32) + b_ref[...]
        for k in range(width):
            acc = acc + pad_ref[pl.ds(base + first + k, rt), :] * w_ref[k:k + 1, :]
        o_ref[0, pl.ds(base, rt), :] = acc.astype(o_ref.dtype)
        return carry

    lax.fori_loop(0, l // rt, body, 0)


def _dwconv(x, w, bias, rt=128):
    b, l, c = x.shape
    width = w.shape[0]
    tc = LANES
    assert width // 2 <= BF16_ROWS
    return pl.pallas_call(
        functools.partial(_dwconv_kernel, width=width, rt=rt),
        grid=(b, c // tc),
        in_specs=[pl.BlockSpec((1, l, tc), lambda bi, j: (bi, 0, j)),
                  pl.BlockSpec((width, tc), lambda bi, j: (0, j)),
                  pl.BlockSpec((1, tc), lambda bi, j: (0, j))],
        out_specs=pl.BlockSpec((1, l, tc), lambda bi, j: (bi, 0, j)),
        out_shape=jax.ShapeDtypeStruct((b, l, c), BF16),
        scratch_shapes=[pltpu.VMEM((l + 2 * BF16_ROWS, tc), F32)],
        compiler_params=_params("parallel", "parallel"),
        name="dwconv",
    )(x, w, bias.reshape(1, c))


FFN_SUB = 256
FFN_NSUB = 2
FFN_ROWS = 1024


def _prep_up_kernel(v_ref, g_ref, o_ref, *, fs, ns):
    for s in range(ns):
        o_ref[0, 0, :, 2 * fs * s:2 * fs * s + fs] = v_ref[0, :, fs * s:fs * (s + 1)].astype(BF16)
        o_ref[0, 0, :, 2 * fs * s + fs:2 * fs * (s + 1)] = g_ref[0, :, fs * s:fs * (s + 1)].astype(BF16)


def _prep_up(w):
    depth, d, f2 = w.shape
    fs, ns = FFN_SUB, FFN_NSUB
    fc = fs * ns
    nc = f2 // 2 // fc
    return pl.pallas_call(
        functools.partial(_prep_up_kernel, fs=fs, ns=ns),
        grid=(depth, nc),
        in_specs=[pl.BlockSpec((1, d, fc), lambda l, c: (l, 0, c)),
                  pl.BlockSpec((1, d, fc), lambda l, c: (l, 0, c + nc))],
        out_specs=pl.BlockSpec((1, 1, d, 2 * fc), lambda l, c: (l, c, 0, 0)),
        out_shape=jax.ShapeDtypeStruct((depth, nc, d, 2 * fc), BF16),
        compiler_params=_params("parallel", "parallel"),
        name="prep_up",
    )(w, w)


def _chunk_pairs(t, f, fs):
    lead = t.shape[:-1]
    t = t.reshape(lead + (2, f // fs, fs))
    return jnp.swapaxes(t, -3, -2).reshape(lead + (2 * f,))


F32_ROWS = 8
FFN_EDGE = 16


def _ffn_kernel(*refs, nseg, seg, fs, ns, nc, rc, emit_h):
    it = iter(refs)
    h_ref, x_hbm, wup_ref, wdw_ref, bdw_ref, wdn_ref, gate_ref = (next(it) for _ in range(7))
    if emit_h:
        ng_ref, nsh_ref, nsc_ref = next(it), next(it), next(it)
    o_ref = next(it)
    hn_ref = next(it) if emit_h else None
    u_ref, x_sem = it
    bi = pl.program_id(0)
    i = pl.program_id(1)
    c = pl.program_id(2)
    rows = nseg * seg
    pad = F32_ROWS
    stride = seg + pad
    lt = 2 * fs // LANES

    def seg_rows(k):
        return slice(k * seg, (k + 1) * seg)

    def per_seg(ref, k):
        return ref[0] if ref.shape[1] == 1 else ref[0, k:k + 1, :]

    def x_copy():
        return pltpu.make_async_copy(x_hbm.at[bi, pl.ds(i * rows, rows), :], o_ref.at[0], x_sem)

    @pl.when(c == 0)
    def _():
        x_copy().start()
        for j in range(ns * lt):
            for k in range(nseg + 1):
                u_ref[j, k * stride:k * stride + pad, :] = jnp.zeros((pad, LANES), F32)
        x_copy().wait()

    h = h_ref[0]
    for s in range(ns):
        u = _dot(h, wup_ref[0, 0, :, 2 * fs * s:2 * fs * (s + 1)])
        for t in range(lt):
            for k in range(nseg):
                u_ref[s * lt + t, pad + k * stride:pad + k * stride + seg, :] = u[seg_rows(k), LANES * t:LANES * (t + 1)]
    for s in range(ns):
        def conv(t, k):
            col = 2 * fs * s + LANES * t
            w = wdw_ref[:, col:col + LANES]
            r0 = pad + k * stride
            ut = u_ref.at[s * lt + t]
            return (ut[r0 - 1:r0 - 1 + seg, :] * w[0:1] + ut[r0:r0 + seg, :] * w[1:2]
                    + ut[r0 + 1:r0 + 1 + seg, :] * w[2:3] + bdw_ref[:, col:col + LANES])

        act = jnp.concatenate(
            [jnp.concatenate([(_silu(conv(lt // 2 + t, k)) * conv(t, k)).astype(BF16) for t in range(lt // 2)],
                             axis=1) for k in range(nseg)], axis=0)
        y = _dot(act, wdn_ref[0, fs * s:fs * (s + 1), :])
        if gate_ref.shape[1] == 1:
            o_ref[0] += gate_ref[0] * y
        else:
            for k in range(nseg):
                o_ref[0, seg_rows(k), :] += per_seg(gate_ref, k) * y[seg_rows(k), :]

    if emit_h:
        @pl.when(c == nc - 1)
        def _():
            if nsh_ref.shape[1] == 1:
                def rows_body(t, carry):
                    r = pl.multiple_of(t * rc, rc)
                    xn = o_ref[0, pl.ds(r, rc), :]
                    hn_ref[0, pl.ds(r, rc), :] = _rms_mod(xn, ng_ref[...], nsh_ref[0], nsc_ref[0]).astype(BF16)
                    return carry

                lax.fori_loop(0, rows // rc, rows_body, 0)
            else:
                for k in range(nseg):
                    hn_ref[0, seg_rows(k), :] = _rms_mod(o_ref[0, seg_rows(k), :], ng_ref[...], per_seg(nsh_ref, k),
                                                         per_seg(nsc_ref, k)).astype(BF16)


def _ffn_call(h, x, layer, wup, wdw, bdw, wdn, gate, nxt, nseg, seg, rc=128):
    nb, total, d = h.shape
    _, nc, _, cw = wup.shape
    fs = FFN_SUB
    ns = cw // (2 * fs)
    rows = nseg * seg
    nt = total // rows
    tile = lambda bi, i, c: (bi, i, 0)
    vec = lambda arr: pl.BlockSpec((1, arr.shape[1], d), lambda bi, i, c: (_bsel(arr.shape[0])(bi), 0, 0))
    args = [h, x, wup, wdw, bdw, wdn, gate]
    specs = [pl.BlockSpec((1, rows, d), tile, pipeline_mode=pl.Buffered(1)),
             pl.BlockSpec(memory_space=pl.ANY),
             pl.BlockSpec((1, 1, d, cw), lambda bi, i, c: (layer, c, 0, 0)),
             pl.BlockSpec((3, cw), lambda bi, i, c: (0, c)),
             pl.BlockSpec((1, cw), lambda bi, i, c: (0, c)),
             pl.BlockSpec((1, fs * ns, d), lambda bi, i, c: (layer, c, 0)),
             vec(gate)]
    out_shape = [jax.ShapeDtypeStruct((nb, total, d), F32)]
    out_specs = [pl.BlockSpec((1, rows, d), tile)]
    if nxt is not None:
        args += [nxt[0].reshape(1, d), nxt[1], nxt[2]]
        specs += [pl.BlockSpec((1, d), lambda bi, i, c: (0, 0)), vec(nxt[1]), vec(nxt[2])]
        out_shape.append(jax.ShapeDtypeStruct((nb, total, d), BF16))
        out_specs.append(pl.BlockSpec((1, rows, d), tile))
    out = pl.pallas_call(
        functools.partial(_ffn_kernel, nseg=nseg, seg=seg, fs=fs, ns=ns, nc=nc, rc=min(rc, rows),
                          emit_h=nxt is not None),
        grid=(nb, nt, nc), in_specs=specs, out_specs=out_specs, out_shape=out_shape,
        scratch_shapes=[pltpu.VMEM((ns * 2 * fs // LANES, nseg * (seg + F32_ROWS) + F32_ROWS, LANES), F32),
                        pltpu.SemaphoreType.DMA(())],
        compiler_params=_params("parallel", "parallel", "arbitrary"),
        name="conv_ffn",
    )(*args)
    return out[0], (out[1] if nxt is not None else None)


def _conv_ffn(h, x, layer, wup, wdw, bdw, wdn, gate, nxt=None):
    b, l, d = h.shape
    if l <= FFN_ROWS:
        shared = gate.shape[0] == 1
        nseg = max(n for n in range(1, b + 1) if b % n == 0 and (n == 1 or (shared and n * l <= FFN_ROWS)))
        grp = lambda t: t.reshape(b // nseg, nseg * l, d)
        xo, ho = _ffn_call(grp(h), grp(x), layer, wup, wdw, bdw, wdn, gate, nxt, nseg, l)
        return xo.reshape(b, l, d), (None if ho is None else ho.reshape(b, l, d))

    xo, ho = _ffn_call(h, x, layer, wup, wdw, bdw, wdn, gate, nxt, 1, FFN_ROWS)
    cuts = [j * FFN_ROWS for j in range(1, l // FFN_ROWS)]
    e = FFN_EDGE
    edge = lambda t: jnp.concatenate([t[:, r - e // 2:r + e // 2] for r in cuts], axis=1).reshape(1, -1, t.shape[-1])
    per_edge = lambda v: jnp.broadcast_to(v, (b, len(cuts), d)).reshape(1, b * len(cuts), d)
    nxt_e = None if nxt is None else (nxt[0], per_edge(nxt[1]), per_edge(nxt[2]))
    xe, he = _ffn_call(edge(h), edge(x), layer, wup, wdw, bdw, wdn, per_edge(gate), nxt_e, b * len(cuts), e)
    for n, r in enumerate(cuts):
        mid = lambda t: t.reshape(b, len(cuts), e, d)[:, n, e // 2 - 1:e // 2 + 1]
        xo = lax.dynamic_update_slice(xo, mid(xe), (0, r - 1, 0))
        if ho is not None:
            ho = lax.dynamic_update_slice(ho, mid(he), (0, r - 1, 0))
    return xo, ho


def _fnet_kernel(h_ref, cc_ref, sc_ref, csl_ref, o_ref, pq_ref, *, groups, gw):
    l = h_ref.shape[1]

    @pl.when(pl.program_id(1) == 0)
    def _():
        for g in range(groups):
            hg = h_ref[0, :, g * gw:(g + 1) * gw]
            pq_ref[0:l, g * gw:(g + 1) * gw] = _dot(hg, cc_ref[...]).astype(BF16)
            pq_ref[l:2 * l, g * gw:(g + 1) * gw] = _dot(hg, sc_ref[...]).astype(BF16)

    o_ref[0] = _dot(csl_ref[...], pq_ref[...]).astype(o_ref.dtype)


def _dft_tables(n):
    idx = jnp.arange(n, dtype=jnp.int32)
    ang = ((idx[:, None] * idx[None, :]) % n).astype(F32) * (2.0 * math.pi / n)
    s = 1.0 / math.sqrt(n)
    return jnp.cos(ang) * s, jnp.sin(ang) * s


def _fourier_mix(h, tl=512):
    b, l, d = h.shape
    gw = d // FNET_GROUPS
    tl = min(tl, l)
    cc, sc = _dft_tables(gw)
    cl, sl = _dft_tables(l)
    csl = jnp.concatenate([cl, -sl], axis=1).astype(BF16)
    return pl.pallas_call(
        functools.partial(_fnet_kernel, groups=FNET_GROUPS, gw=gw),
        grid=(b, l // tl),
        in_specs=[pl.BlockSpec((1, l, d), lambda bi, i: (bi, 0, 0)),
                  pl.BlockSpec((gw, gw), lambda bi, i: (0, 0)),
                  pl.BlockSpec((gw, gw), lambda bi, i: (0, 0)),
                  pl.BlockSpec((tl, 2 * l), lambda bi, i: (i, 0))],
        out_specs=pl.BlockSpec((1, tl, d), lambda bi, i: (bi, i, 0)),
        out_shape=jax.ShapeDtypeStruct((b, l, d), BF16),
        scratch_shapes=[pltpu.VMEM((2 * l, d), BF16)],
        compiler_params=_params("parallel", "arbitrary"),
        name="fourier_mix",
    )(h, cc.astype(BF16), sc.astype(BF16), csl)


NA_QROWS = 4
NA_KROWS = 12


def _na_plan(rows):
    assert rows % NA_QROWS == 0 and rows >= NA_KROWS
    patterns, blocks = [], []
    for j in range(rows // NA_QROWS):
        r0 = j * NA_QROWS
        ks = min(max(r0 - WIN_H // 2, 0), rows - NA_KROWS)
        pat = []
        for ql in range(NA_QROWS):
            r = r0 + ql
            rs = min(max(r - WIN_H // 2, 0), rows - WIN_H)
            assert ks <= rs and rs + WIN_H <= ks + NA_KROWS
            pat.append((rs - ks, ks - r + WIN_H - 1))
        pat = tuple(pat)
        if pat not in patterns:
            patterns.append(pat)
        blocks.append((ks, patterns.index(pat)))
    return tuple(blocks), tuple(patterns)


def _na_kernel(rpb_ref, q_ref, k_ref, v_ref, kc_ref, vc_ref, qg_ref, kg_ref, o_ref,
               base_ref, tab_ref, qn_ref, kn_ref, kcn_ref, *, blocks, patterns, hd):
    w = GRID_W
    ndr, ndc = 2 * WIN_H - 1, 2 * WIN_W - 1
    head = pl.program_id(0)

    @pl.when(pl.program_id(1) == 0)
    def _build_bias_tables():
        qi = lax.broadcasted_iota(jnp.int32, (w, w), 0)
        ki = lax.broadcasted_iota(jnp.int32, (w, w), 1)
        cstart = jnp.clip(qi - WIN_W // 2, 0, w - WIN_W)
        in_window = (ki >= cstart) & (ki < cstart + WIN_W)
        dci = jnp.clip(ki - qi, -(WIN_W - 1), WIN_W - 1) + (WIN_W - 1)
        for dr in range(ndr):
            t = jnp.zeros((w, w), F32)
            for dc in range(ndc):
                t = jnp.where(dci == dc, rpb_ref[(head * ndr + dr) * ndc + dc], t)
            base_ref[dr] = jnp.where(in_window, t, MASKED)
        for p, pat in enumerate(patterns):
            for ql, (first, droff) in enumerate(pat):
                for kl in range(NA_KROWS):
                    if first <= kl < first + WIN_H:
                        piece = base_ref[kl + droff]
                    else:
                        piece = jnp.full((w, w), MASKED, F32)
                    tab_ref[p, ql * w:(ql + 1) * w, kl * w:(kl + 1) * w] = piece

    def rms(t, g):
        t = t.astype(F32)
        return t * lax.rsqrt(jnp.mean(t * t, axis=-1, keepdims=True) + EPS) * g

    qn_ref[...] = (rms(q_ref[0], qg_ref[...]) * (hd ** -0.5)).astype(BF16)
    kn_ref[...] = rms(k_ref[0], kg_ref[...]).astype(BF16)
    kcn_ref[...] = rms(kc_ref[0], kg_ref[...]).astype(BF16)

    nt = (((1,), (1,)), ((), ()))
    nq, nk = NA_QROWS * w, NA_KROWS * w
    for j, (ks, p) in enumerate(blocks):
        qb = qn_ref[j * nq:(j + 1) * nq, :]
        s1 = lax.dot_general(qb, kn_ref[ks * w:ks * w + nk, :], nt, preferred_element_type=F32) + tab_ref[p]
        s2 = lax.dot_general(qb, kcn_ref[...], nt, preferred_element_type=F32)
        mx = jnp.maximum(jnp.max(s1, axis=-1, keepdims=True), jnp.max(s2, axis=-1, keepdims=True))
        p1 = jnp.exp(s1 - mx)
        p2 = jnp.exp(s2 - mx)
        den = jnp.sum(p1, axis=-1, keepdims=True) + jnp.sum(p2, axis=-1, keepdims=True)
        o = _dot(p1.astype(BF16), v_ref[0, ks * w:ks * w + nk, :]) + _dot(p2.astype(BF16), vc_ref[0])
        o_ref[0, j * nq:(j + 1) * nq, :] = (o / den).astype(o_ref.dtype)


def _neighbourhood_attention(qkv, qkvc, q_g, k_g, rpb):
    b, l, d3 = qkv.shape
    d = d3 // 3
    nctx = qkvc.shape[1]
    hd = d // NA_HEADS
    rows = l // GRID_W
    blocks, patterns = _na_plan(rows)
    nq, nk = NA_QROWS * GRID_W, NA_KROWS * GRID_W
    seq = lambda col0: pl.BlockSpec((1, l, hd), lambda h, bi: (bi, 0, col0 + h))
    ctx = lambda col0: pl.BlockSpec((1, nctx, hd), lambda h, bi: (bi, 0, col0 + h))
    gain = pl.BlockSpec((1, hd), lambda h, bi: (0, 0))
    return pl.pallas_call(
        functools.partial(_na_kernel, blocks=blocks, patterns=patterns, hd=hd),
        grid=(NA_HEADS, b),
        in_specs=[pl.BlockSpec(memory_space=pltpu.SMEM),
                  seq(0), seq(NA_HEADS), seq(2 * NA_HEADS), ctx(NA_HEADS), ctx(2 * NA_HEADS), gain, gain],
        out_specs=pl.BlockSpec((1, l, hd), lambda h, bi: (bi, 0, h)),
        out_shape=jax.ShapeDtypeStruct((b, l, d), BF16),
        scratch_shapes=[pltpu.VMEM((2 * WIN_H - 1, GRID_W, GRID_W), F32),
                        pltpu.VMEM((len(patterns), nq, nk), F32),
                        pltpu.VMEM((l, hd), BF16), pltpu.VMEM((l, hd), BF16), pltpu.VMEM((nctx, hd), BF16)],
        compiler_params=_params("arbitrary", "arbitrary"),
        name="neighbourhood_attention",
    )(rpb.reshape(-1), qkv, qkv, qkv, qkvc, qkvc, q_g.reshape(1, hd), k_g.reshape(1, hd))


def kernel(x, c, ctx, c_ctx, mod_w, mod_b, norm1_g, norm2_g, a_w_pw1, a_b_pw1, a_w_dw, a_b_dw, a_ln_g, a_ln_b,
           a_w_pw2, a_b_pw2, b_w_out, b_b_out, c_w_qkv, c_q_g, c_k_g, c_rpb, c_w_o, f_w_up, f_w_dw, f_b_dw, f_w_down):
    bsz, seq, d = x.shape
    nctx = ctx.shape[1]
    depth = mod_w.shape[0]
    f = f_w_down.shape[1]
    n_mix = 3

    cond = jnp.zeros((MOD_ROWS, d), F32).at[:bsz].set(c).at[bsz].set(c_ctx)
    mods = _modulation(cond, mod_w, mod_b)

    def mod_vecs(i, first_row, n_rows):
        m = mods[i, first_row:first_row + n_rows]
        return [m[:, None, k * d:(k + 1) * d] for k in range(6)]

    ctx_out = [any(l % n_mix == 2 for l in range(i + 1, depth)) for i in range(depth)]
    ctx_in = [ctx_out[i] or i % n_mix == 2 for i in range(depth)]

    flat = lambda t: t.reshape(1, bsz * nctx, t.shape[-1])
    unflat = lambda t: t.reshape(bsz, nctx, t.shape[-1])

    wup_all = _prep_up(f_w_up)
    wdn_all = f_w_down.astype(BF16)

    mx, mc = mod_vecs(0, 0, bsz), mod_vecs(0, bsz, 1)
    h = _normmod(x, norm1_g[0], mx[0], mx[1])
    hc = _normmod(ctx, norm1_g[0], mc[0], mc[1]) if ctx_in[0] else None

    for i in range(depth):
        kind, j = i % n_mix, i // n_mix
        mx, mc = mod_vecs(i, 0, bsz), mod_vecs(i, bsz, 1)
        last = i + 1 == depth
        nmx, nmc = (None, None) if last else (mod_vecs(i + 1, 0, bsz), mod_vecs(i + 1, bsz, 1))

        if kind == 0:
            w1, w2 = a_w_pw1[j].astype(BF16), a_w_pw2[j].astype(BF16)

            def mixer(hh, res, gate, nxt, shared):
                fl, un = (flat, unflat) if shared else ((lambda t: t), (lambda t: t))
                u = un(_matmul_glu(fl(hh), w1, a_b_pw1[j]))
                u = _dwconv(u, a_w_dw[j], a_b_dw[j])
                xo, ho = _gated_matmul(fl(u), w2, a_b_pw2[j], fl(res), gate, ln=(a_ln_g[j], a_ln_b[j]), nxt=nxt)
                return un(xo), un(ho)
        elif kind == 1:
            wo = b_w_out[j].astype(BF16)

            def mixer(hh, res, gate, nxt, shared):
                fl, un = (flat, unflat) if shared else ((lambda t: t), (lambda t: t))
                xo, ho = _gated_matmul(fl(_fourier_mix(hh)), wo, b_b_out[j], fl(res), gate, nxt=nxt)
                return un(xo), un(ho)
        else:
            wqkv, wo = c_w_qkv[j].astype(BF16), c_w_o[j].astype(BF16)
            qkvc = unflat(_matmul(flat(hc), wqkv))

            def mixer(hh, res, gate, nxt, shared):
                assert not shared, "context outputs of the attention mixer are never consumed"
                o = _neighbourhood_attention(_matmul(hh, wqkv), qkvc, c_q_g[j], c_k_g[j], c_rpb[j])
                return _gated_matmul(o, wo, jnp.zeros((d,), F32), res, gate, nxt=nxt)

        wdw = _chunk_pairs(f_w_dw[i], f, FFN_SUB)
        bdw = _chunk_pairs(f_b_dw[i], f, FFN_SUB).reshape(1, 2 * f)

        x, h2 = mixer(h, x, mx[2], (norm2_g[i], mx[3], mx[4]), False)
        x, h = _conv_ffn(h2, x, i, wup_all, wdw, bdw, wdn_all, mx[5],
                         nxt=None if last else (norm1_g[i + 1], nmx[0], nmx[1]))
        if ctx_out[i]:
            ctx, hc2 = mixer(hc, ctx, mc[2], (norm2_g[i], mc[3], mc[4]), True)
            ctx, hc = _conv_ffn(hc2, ctx, i, wup_all, wdw, bdw, wdn_all, mc[5],
                                nxt=(norm1_g[i + 1], nmc[0], nmc[1]) if ctx_in[i + 1] else None)
    return x
```

```python
import functools
import math

import jax
import jax.numpy as jnp
from jax import lax
from jax.experimental import pallas as pl
from jax.experimental.pallas import tpu as pltpu

GRID_W = 64
FNET_GROUPS = 8
NA_HEADS = 16
WIN_H = 8
WIN_W = 16
EPS = 1e-6

F32 = jnp.float32
BF16 = jnp.bfloat16
MASKED = -1e30
V7X_VMEM_BYTES = 64 * 1024 * 1024
VMEM_LIMIT = V7X_VMEM_BYTES - 6 * 1024 * 1024
BF16_ROWS = 16
LANES = 128
MOD_ROWS = 32


def _params(*sem):
    return pltpu.CompilerParams(dimension_semantics=sem, vmem_limit_bytes=VMEM_LIMIT)


def _dot(a, b):
    return jnp.dot(a, b, preferred_element_type=F32)


def _silu(t):
    return t * jax.nn.sigmoid(t)


def _rms_mod(x, g, shift, scale):
    y = x * lax.rsqrt(jnp.mean(x * x, axis=-1, keepdims=True) + EPS) * g
    return y * (1.0 + scale) + shift


def _fit(n, pref, unit=LANES):
    t = min(pref, n) // unit * unit
    while n % t:
        t -= unit
    return t


def _bsel(n):
    return (lambda b: b) if n > 1 else (lambda b: 0)


def _mod_kernel(c_ref, w_ref, b_ref, o_ref):
    s = _silu(c_ref[...]).astype(BF16)
    o_ref[0] = _dot(s, w_ref[0].astype(BF16)) + b_ref[0]


def _modulation(cond, mod_w, mod_b, tn=1024):
    depth, d, n = mod_w.shape
    rows = cond.shape[0]
    tn = _fit(n, tn)
    return pl.pallas_call(
        _mod_kernel,
        grid=(depth, n // tn),
        in_specs=[pl.BlockSpec((rows, d), lambda l, j: (0, 0)),
                  pl.BlockSpec((1, d, tn), lambda l, j: (l, 0, j)),
                  pl.BlockSpec((1, 1, tn), lambda l, j: (l, 0, j))],
        out_specs=pl.BlockSpec((1, rows, tn), lambda l, j: (l, 0, j)),
        out_shape=jax.ShapeDtypeStruct((depth, rows, n), F32),
        compiler_params=_params("parallel", "parallel"),
        name="modulation",
    )(cond, mod_w, mod_b.reshape(depth, 1, n))


def _normmod_kernel(x_ref, g_ref, sh_ref, sc_ref, o_ref):
    o_ref[0] = _rms_mod(x_ref[0], g_ref[...], sh_ref[0], sc_ref[0]).astype(BF16)


def _normmod(x, g, shift, scale, tl=512):
    b, l, d = x.shape
    tl = min(tl, l)
    sel = _bsel(shift.shape[0])
    vec = pl.BlockSpec((1, 1, d), lambda bi, i: (sel(bi), 0, 0))
    return pl.pallas_call(
        _normmod_kernel,
        grid=(b, l // tl),
        in_specs=[pl.BlockSpec((1, tl, d), lambda bi, i: (bi, i, 0)),
                  pl.BlockSpec((1, d), lambda bi, i: (0, 0)), vec, vec],
        out_specs=pl.BlockSpec((1, tl, d), lambda bi, i: (bi, i, 0)),
        out_shape=jax.ShapeDtypeStruct((b, l, d), BF16),
        compiler_params=_params("parallel", "parallel"),
        name="normmod",
    )(x, g.reshape(1, d), shift, scale)


def _mm_kernel(a_ref, w_ref, o_ref):
    o_ref[0] = _dot(a_ref[0], w_ref[...]).astype(o_ref.dtype)


def _glu_kernel(a_ref, wa_ref, wg_ref, ba_ref, bg_ref, o_ref):
    a = a_ref[0]
    ua = _dot(a, wa_ref[...]) + ba_ref[...]
    ug = _dot(a, wg_ref[...]) + bg_ref[...]
    o_ref[0] = (ua * jax.nn.sigmoid(ug)).astype(o_ref.dtype)


def _matmul(a, w, tm=2048, tn=512):
    b, l, k = a.shape
    n = w.shape[1]
    tm, tn = min(tm, l), _fit(n, tn)
    return pl.pallas_call(
        _mm_kernel,
        grid=(b, l // tm, n // tn),
        in_specs=[pl.BlockSpec((1, tm, k), lambda bi, i, j: (bi, i, 0)),
                  pl.BlockSpec((k, tn), lambda bi, i, j: (0, j))],
        out_specs=pl.BlockSpec((1, tm, tn), lambda bi, i, j: (bi, i, j)),
        out_shape=jax.ShapeDtypeStruct((b, l, n), BF16),
        compiler_params=_params("parallel", "parallel", "arbitrary"),
        name="matmul",
    )(a, w)


def _matmul_glu(a, w, bias, tm=2048, tn=512):
    b, l, k = a.shape
    n = w.shape[1] // 2
    tm, tn = min(tm, l), _fit(n, tn)
    nj = n // tn
    bias = bias.reshape(1, 2 * n)
    return pl.pallas_call(
        _glu_kernel,
        grid=(b, l // tm, nj),
        in_specs=[pl.BlockSpec((1, tm, k), lambda bi, i, j: (bi, i, 0)),
                  pl.BlockSpec((k, tn), lambda bi, i, j: (0, j)),
                  pl.BlockSpec((k, tn), lambda bi, i, j: (0, j + nj)),
                  pl.BlockSpec((1, tn), lambda bi, i, j: (0, j)),
                  pl.BlockSpec((1, tn), lambda bi, i, j: (0, j + nj))],
        out_specs=pl.BlockSpec((1, tm, tn), lambda bi, i, j: (bi, i, j)),
        out_shape=jax.ShapeDtypeStruct((b, l, n), BF16),
        compiler_params=_params("parallel", "parallel", "arbitrary"),
        name="matmul_glu",
    )(a, w, w, bias, bias)


GATED_SUB = 256


def _gated_kernel(*refs, ln, emit_h, nsub):
    it = iter(refs)
    a_ref, w_ref, bias_ref, res_ref, gate_ref = (next(it) for _ in range(5))
    if ln:
        lng_ref, lnb_ref = next(it), next(it)
    if emit_h:
        ng_ref, nsh_ref, nsc_ref = next(it), next(it), next(it)
    o_ref = next(it)
    h_ref = next(it) if emit_h else None
    sub = a_ref.shape[1] // nsub

    def prologue(r):
        a = a_ref[0, r * sub:(r + 1) * sub, :]
        if not ln:
            return a
        t = a.astype(F32)
        mu = jnp.mean(t, axis=-1, keepdims=True)
        var = jnp.mean(jnp.square(t - mu), axis=-1, keepdims=True)
        t = (t - mu) * lax.rsqrt(var + EPS) * lng_ref[...] + lnb_ref[...]
        return _silu(t).astype(BF16)

    lhs = [prologue(r) for r in range(nsub)]
    for r in range(nsub):
        rows = slice(r * sub, (r + 1) * sub)
        xn = res_ref[0, rows, :] + gate_ref[0] * (_dot(lhs[r], w_ref[...]) + bias_ref[...])
        o_ref[0, rows, :] = xn
        if emit_h:
            h_ref[0, rows, :] = _rms_mod(xn, ng_ref[...], nsh_ref[0], nsc_ref[0]).astype(BF16)


def _gated_matmul(a, w, bias, res, gate, ln=None, nxt=None, tm=512):
    b, l, k = a.shape
    d = w.shape[1]
    tm = min(tm, l)
    nsub = max(tm // GATED_SUB, 1)
    row = lambda bi, i: (bi, i, 0)
    const = lambda bi, i: (0, 0)

    def vec(arr):
        sel = _bsel(arr.shape[0])
        return pl.BlockSpec((1, 1, d), lambda bi, i: (sel(bi), 0, 0))

    args = [a, w, bias.reshape(1, d), res, gate]
    specs = [pl.BlockSpec((1, tm, k), row),
             pl.BlockSpec((k, d), const, pipeline_mode=pl.Buffered(1)),
             pl.BlockSpec((1, d), const), pl.BlockSpec((1, tm, d), row), vec(gate)]
    if ln is not None:
        args += [ln[0].reshape(1, k), ln[1].reshape(1, k)]
        specs += [pl.BlockSpec((1, k), const)] * 2
    out_shape = [jax.ShapeDtypeStruct((b, l, d), F32)]
    out_specs = [pl.BlockSpec((1, tm, d), row)]
    if nxt is not None:
        args += [nxt[0].reshape(1, d), nxt[1], nxt[2]]
        specs += [pl.BlockSpec((1, d), const), vec(nxt[1]), vec(nxt[2])]
        out_shape.append(jax.ShapeDtypeStruct((b, l, d), BF16))
        out_specs.append(pl.BlockSpec((1, tm, d), row))
    out = pl.pallas_call(
        functools.partial(_gated_kernel, ln=ln is not None, emit_h=nxt is not None, nsub=nsub),
        grid=(b, l // tm), in_specs=specs, out_specs=out_specs, out_shape=out_shape,
        compiler_params=_params("parallel", "parallel"),
        name="gated_matmul",
    )(*args)
    return (out[0], out[1]) if nxt is not None else (out[0], None)


def _dwconv_kernel(x_ref, w_ref, b_ref, o_ref, pad_ref, *, width, rt):
    l, tc = x_ref.shape[1], x_ref.shape[2]
    halo = BF16_ROWS
    first = halo - width // 2
    pad_ref[0:halo, :] = jnp.zeros((halo, tc), F32)
    pad_ref[halo + l:halo + l + halo, :] = jnp.zeros((halo, tc), F32)
    pad_ref[halo:halo + l, :] = x_ref[0].astype(F32)

    def body(t, carry):
        base = pl.multiple_of(t * rt, rt)
        acc = jnp.zeros((rt, tc), F32) + b_ref[...]
        for k in range(width):
            acc = acc + pad_ref[pl.ds(base + first + k, rt), :] * w_ref[k:k + 1, :]
        o_ref[0, pl.ds(base, rt), :] = acc.astype(o_ref.dtype)
        return carry

    lax.fori_loop(0, l // rt, body, 0)


def _dwconv(x, w, bias, rt=512):
    b, l, c = x.shape
    width = w.shape[0]
    tc = LANES
    rt = min(rt, l)
    assert width // 2 <= BF16_ROWS and l % rt == 0
    return pl.pallas_call(
        functools.partial(_dwconv_kernel, width=width, rt=rt),
        grid=(b, c // tc),
        in_specs=[pl.BlockSpec((1, l, tc), lambda bi, j: (bi, 0, j)),
                  pl.BlockSpec((width, tc), lambda bi, j: (0, j)),
                  pl.BlockSpec((1, tc), lambda bi, j: (0, j))],
        out_specs=pl.BlockSpec((1, l, tc), lambda bi, j: (bi, 0, j)),
        out_shape=jax.ShapeDtypeStruct((b, l, c), BF16),
        scratch_shapes=[pltpu.VMEM((l + 2 * BF16_ROWS, tc), F32)],
        compiler_params=_params("parallel", "parallel"),
        name="dwconv",
    )(x, w, bias.reshape(1, c))


FFN_SUB = 256
FFN_NSUB = 2
FFN_ROWS = 1024


def _prep_up_kernel(v_ref, g_ref, o_ref, *, fs, ns):
    for s in range(ns):
        o_ref[0, 0, :, 2 * fs * s:2 * fs * s + fs] = v_ref[0, :, fs * s:fs * (s + 1)].astype(BF16)
        o_ref[0, 0, :, 2 * fs * s + fs:2 * fs * (s + 1)] = g_ref[0, :, fs * s:fs * (s + 1)].astype(BF16)


def _prep_up(w):
    depth, d, f2 = w.shape
    fs, ns = FFN_SUB, FFN_NSUB
    fc = fs * ns
    nc = f2 // 2 // fc
    return pl.pallas_call(
        functools.partial(_prep_up_kernel, fs=fs, ns=ns),
        grid=(depth, nc),
        in_specs=[pl.BlockSpec((1, d, fc), lambda l, c: (l, 0, c)),
                  pl.BlockSpec((1, d, fc), lambda l, c: (l, 0, c + nc))],
        out_specs=pl.BlockSpec((1, 1, d, 2 * fc), lambda l, c: (l, c, 0, 0)),
        out_shape=jax.ShapeDtypeStruct((depth, nc, d, 2 * fc), BF16),
        compiler_params=_params("parallel", "parallel"),
        name="prep_up",
    )(w, w)


def _chunk_pairs(t, f, fs):
    lead = t.shape[:-1]
    t = t.reshape(lead + (2, f // fs, fs))
    return jnp.swapaxes(t, -3, -2).reshape(lead + (2 * f,))


F32_ROWS = 8
FFN_EDGE = 16


def _ffn_kernel(*refs, nseg, seg, fs, ns, nc, rc, emit_h):
    it = iter(refs)
    h_ref, x_hbm, wup_ref, wdw_ref, bdw_ref, wdn_ref, gate_ref = (next(it) for _ in range(7))
    if emit_h:
        ng_ref, nsh_ref, nsc_ref = next(it), next(it), next(it)
    o_ref = next(it)
    hn_ref = next(it) if emit_h else None
    u_ref, x_sem = it
    bi = pl.program_id(0)
    i = pl.program_id(1)
    c = pl.program_id(2)
    rows = nseg * seg
    pad = F32_ROWS
    stride = seg + pad
    lt = 2 * fs // LANES

    def seg_rows(k):
        return slice(k * seg, (k + 1) * seg)

    def per_seg(ref, k):
        return ref[0] if ref.shape[1] == 1 else ref[0, k:k + 1, :]

    def x_copy():
        return pltpu.make_async_copy(x_hbm.at[bi, pl.ds(i * rows, rows), :], o_ref.at[0], x_sem)

    @pl.when(c == 0)
    def _():
        x_copy().start()
        for j in range(ns * lt):
            for k in range(nseg + 1):
                u_ref[j, k * stride:k * stride + pad, :] = jnp.zeros((pad, LANES), F32)
        x_copy().wait()

    h = h_ref[0]
    for s in range(ns):
        u = _dot(h, wup_ref[0, 0, :, 2 * fs * s:2 * fs * (s + 1)])
        for t in range(lt):
            for k in range(nseg):
                u_ref[s * lt + t, pad + k * stride:pad + k * stride + seg, :] = u[seg_rows(k), LANES * t:LANES * (t + 1)]
    for s in range(ns):
        def conv(t, k):
            col = 2 * fs * s + LANES * t
            w = wdw_ref[:, col:col + LANES]
            r0 = pad + k * stride
            ut = u_ref.at[s * lt + t]
            return (ut[r0 - 1:r0 - 1 + seg, :] * w[0:1] + ut[r0:r0 + seg, :] * w[1:2]
                    + ut[r0 + 1:r0 + 1 + seg, :] * w[2:3] + bdw_ref[:, col:col + LANES])

        act = jnp.concatenate(
            [jnp.concatenate([(_silu(conv(lt // 2 + t, k)) * conv(t, k)).astype(BF16) for t in range(lt // 2)],
                             axis=1) for k in range(nseg)], axis=0)
        y = _dot(act, wdn_ref[0, fs * s:fs * (s + 1), :])
        if gate_ref.shape[1] == 1:
            o_ref[0] += gate_ref[0] * y
        else:
            for k in range(nseg):
                o_ref[0, seg_rows(k), :] += per_seg(gate_ref, k) * y[seg_rows(k), :]

    if emit_h:
        @pl.when(c == nc - 1)
        def _():
            if nsh_ref.shape[1] == 1:
                def rows_body(t, carry):
                    r = pl.multiple_of(t * rc, rc)
                    xn = o_ref[0, pl.ds(r, rc), :]
                    hn_ref[0, pl.ds(r, rc), :] = _rms_mod(xn, ng_ref[...], nsh_ref[0], nsc_ref[0]).astype(BF16)
                    return carry

                lax.fori_loop(0, rows // rc, rows_body, 0)
            else:
                for k in range(nseg):
                    hn_ref[0, seg_rows(k), :] = _rms_mod(o_ref[0, seg_rows(k), :], ng_ref[...], per_seg(nsh_ref, k),
                                                         per_seg(nsc_ref, k)).astype(BF16)


def _ffn_call(h, x, layer, wup, wdw, bdw, wdn, gate, nxt, nseg, seg, rc=128):
    nb, total, d = h.shape
    _, nc, _, cw = wup.shape
    fs = FFN_SUB
    ns = cw // (2 * fs)
    rows = nseg * seg
    nt = total // rows
    tile = lambda bi, i, c: (bi, i, 0)
    vec = lambda arr: pl.BlockSpec((1, arr.shape[1], d), lambda bi, i, c: (_bsel(arr.shape[0])(bi), 0, 0))
    args = [h, x, wup, wdw, bdw, wdn, gate]
    specs = [pl.BlockSpec((1, rows, d), tile),
             pl.BlockSpec(memory_space=pl.ANY),
             pl.BlockSpec((1, 1, d, cw), lambda bi, i, c: (layer, c, 0, 0)),
             pl.BlockSpec((3, cw), lambda bi, i, c: (0, c)),
             pl.BlockSpec((1, cw), lambda bi, i, c: (0, c)),
             pl.BlockSpec((1, fs * ns, d), lambda bi, i, c: (layer, c, 0)),
             vec(gate)]
    out_shape = [jax.ShapeDtypeStruct((nb, total, d), F32)]
    out_specs = [pl.BlockSpec((1, rows, d), tile)]
    if nxt is not None:
        args += [nxt[0].reshape(1, d), nxt[1], nxt[2]]
        specs += [pl.BlockSpec((1, d), lambda bi, i, c: (0, 0)), vec(nxt[1]), vec(nxt[2])]
        out_shape.append(jax.ShapeDtypeStruct((nb, total, d), BF16))
        out_specs.append(pl.BlockSpec((1, rows, d), tile))
    out = pl.pallas_call(
        functools.partial(_ffn_kernel, nseg=nseg, seg=seg, fs=fs, ns=ns, nc=nc, rc=min(rc, rows),
                          emit_h=nxt is not None),
        grid=(nb, nt, nc), in_specs=specs, out_specs=out_specs, out_shape=out_shape,
        scratch_shapes=[pltpu.VMEM((ns * 2 * fs // LANES, nseg * (seg + F32_ROWS) + F32_ROWS, LANES), F32),
                        pltpu.SemaphoreType.DMA(())],
        compiler_params=_params("parallel", "parallel", "arbitrary"),
        name="conv_ffn",
    )(*args)
    return out[0], (out[1] if nxt is not None else None)


def _conv_ffn(h, x, layer, wup, wdw, bdw, wdn, gate, nxt=None):
    b, l, d = h.shape
    if l <= FFN_ROWS:
        shared = gate.shape[0] == 1
        nseg = max(n for n in range(1, b + 1) if b % n == 0 and (n == 1 or (shared and n * l <= FFN_ROWS)))
        grp = lambda t: t.reshape(b // nseg, nseg * l, d)
        xo, ho = _ffn_call(grp(h), grp(x), layer, wup, wdw, bdw, wdn, gate, nxt, nseg, l)
        return xo.reshape(b, l, d), (None if ho is None else ho.reshape(b, l, d))

    xo, ho = _ffn_call(h, x, layer, wup, wdw, bdw, wdn, gate, nxt, 1, FFN_ROWS)
    cuts = [j * FFN_ROWS for j in range(1, l // FFN_ROWS)]
    e = FFN_EDGE
    edge = lambda t: jnp.concatenate([t[:, r - e // 2:r + e // 2] for r in cuts], axis=1).reshape(1, -1, t.shape[-1])
    per_edge = lambda v: jnp.broadcast_to(v, (b, len(cuts), d)).reshape(1, b * len(cuts), d)
    nxt_e = None if nxt is None else (nxt[0], per_edge(nxt[1]), per_edge(nxt[2]))
    xe, he = _ffn_call(edge(h), edge(x), layer, wup, wdw, bdw, wdn, per_edge(gate), nxt_e, b * len(cuts), e)
    for n, r in enumerate(cuts):
        mid = lambda t: t.reshape(b, len(cuts), e, d)[:, n, e // 2 - 1:e // 2 + 1]
        xo = lax.dynamic_update_slice(xo, mid(xe), (0, r - 1, 0))
        if ho is not None:
            ho = lax.dynamic_update_slice(ho, mid(he), (0, r - 1, 0))
    return xo, ho


def _fnet_kernel(h_ref, cc_ref, sc_ref, csl_ref, o_ref, pq_ref, *, groups, gw):
    l = h_ref.shape[1]

    @pl.when(pl.program_id(1) == 0)
    def _():
        for g in range(groups):
            hg = h_ref[0, :, g * gw:(g + 1) * gw]
            pq_ref[0:l, g * gw:(g + 1) * gw] = _dot(hg, cc_ref[...]).astype(BF16)
            pq_ref[l:2 * l, g * gw:(g + 1) * gw] = _dot(hg, sc_ref[...]).astype(BF16)

    o_ref[0] = _dot(csl_ref[...], pq_ref[...]).astype(o_ref.dtype)


def _dft_tables(n):
    idx = jnp.arange(n, dtype=jnp.int32)
    ang = ((idx[:, None] * idx[None, :]) % n).astype(F32) * (2.0 * math.pi / n)
    s = 1.0 / math.sqrt(n)
    return jnp.cos(ang) * s, jnp.sin(ang) * s


def _fourier_mix(h, tl=512):
    b, l, d = h.shape
    gw = d // FNET_GROUPS
    tl = min(tl, l)
    cc, sc = _dft_tables(gw)
    cl, sl = _dft_tables(l)
    csl = jnp.concatenate([cl, -sl], axis=1).astype(BF16)
    return pl.pallas_call(
        functools.partial(_fnet_kernel, groups=FNET_GROUPS, gw=gw),
        grid=(b, l // tl),
        in_specs=[pl.BlockSpec((1, l, d), lambda bi, i: (bi, 0, 0)),
                  pl.BlockSpec((gw, gw), lambda bi, i: (0, 0)),
                  pl.BlockSpec((gw, gw), lambda bi, i: (0, 0)),
                  pl.BlockSpec((tl, 2 * l), lambda bi, i: (i, 0))],
        out_specs=pl.BlockSpec((1, tl, d), lambda bi, i: (bi, i, 0)),
        out_shape=jax.ShapeDtypeStruct((b, l, d), BF16),
        scratch_shapes=[pltpu.VMEM((2 * l, d), BF16)],
        compiler_params=_params("parallel", "arbitrary"),
        name="fourier_mix",
    )(h, cc.astype(BF16), sc.astype(BF16), csl)


NA_QROWS = 4
NA_KROWS = 12


def _na_plan(rows):
    assert rows % NA_QROWS == 0 and rows >= NA_KROWS
    patterns, blocks = [], []
    for j in range(rows // NA_QROWS):
        r0 = j * NA_QROWS
        ks = min(max(r0 - WIN_H // 2, 0), rows - NA_KROWS)
        pat = []
        for ql in range(NA_QROWS):
            r = r0 + ql
            rs = min(max(r - WIN_H // 2, 0), rows - WIN_H)
            assert ks <= rs and rs + WIN_H <= ks + NA_KROWS
            pat.append((rs - ks, ks - r + WIN_H - 1))
        pat = tuple(pat)
        if pat not in patterns:
            patterns.append(pat)
        blocks.append((ks, patterns.index(pat)))
    return tuple(blocks), tuple(patterns)


def _na_kernel(rpb_ref, q_ref, k_ref, v_ref, kc_ref, vc_ref, qg_ref, kg_ref, o_ref,
               base_ref, tab_ref, qn_ref, kn_ref, kcn_ref, *, blocks, patterns, hd):
    w = GRID_W
    ndr, ndc = 2 * WIN_H - 1, 2 * WIN_W - 1
    head = pl.program_id(0)

    @pl.when(pl.program_id(1) == 0)
    def _build_bias_tables():
        qi = lax.broadcasted_iota(jnp.int32, (w, w), 0)
        ki = lax.broadcasted_iota(jnp.int32, (w, w), 1)
        cstart = jnp.clip(qi - WIN_W // 2, 0, w - WIN_W)
        in_window = (ki >= cstart) & (ki < cstart + WIN_W)
        dci = jnp.clip(ki - qi, -(WIN_W - 1), WIN_W - 1) + (WIN_W - 1)
        for dr in range(ndr):
            t = jnp.zeros((w, w), F32)
            for dc in range(ndc):
                t = jnp.where(dci == dc, rpb_ref[(head * ndr + dr) * ndc + dc], t)
            base_ref[dr] = jnp.where(in_window, t, MASKED)
        for p, pat in enumerate(patterns):
            for ql, (first, droff) in enumerate(pat):
                for kl in range(NA_KROWS):
                    if first <= kl < first + WIN_H:
                        piece = base_ref[kl + droff]
                    else:
                        piece = jnp.full((w, w), MASKED, F32)
                    tab_ref[p, ql * w:(ql + 1) * w, kl * w:(kl + 1) * w] = piece

    def rms(t, g):
        t = t.astype(F32)
        return t * lax.rsqrt(jnp.mean(t * t, axis=-1, keepdims=True) + EPS) * g

    qn_ref[...] = (rms(q_ref[0], qg_ref[...]) * (hd ** -0.5)).astype(BF16)
    kn_ref[...] = rms(k_ref[0], kg_ref[...]).astype(BF16)
    kcn_ref[...] = rms(kc_ref[0], kg_ref[...]).astype(BF16)

    nt = (((1,), (1,)), ((), ()))
    nq, nk = NA_QROWS * w, NA_KROWS * w
    for j, (ks, p) in enumerate(blocks):
        qb = qn_ref[j * nq:(j + 1) * nq, :]
        s1 = lax.dot_general(qb, kn_ref[ks * w:ks * w + nk, :], nt, preferred_element_type=F32) + tab_ref[p]
        s2 = lax.dot_general(qb, kcn_ref[...], nt, preferred_element_type=F32)
        mx = jnp.maximum(jnp.max(s1, axis=-1, keepdims=True), jnp.max(s2, axis=-1, keepdims=True))
        p1 = jnp.exp(s1 - mx)
        p2 = jnp.exp(s2 - mx)
        den = jnp.sum(p1, axis=-1, keepdims=True) + jnp.sum(p2, axis=-1, keepdims=True)
        o = _dot(p1.astype(BF16), v_ref[0, ks * w:ks * w + nk, :]) + _dot(p2.astype(BF16), vc_ref[0])
        o_ref[0, j * nq:(j + 1) * nq, :] = (o / den).astype(o_ref.dtype)


def _neighbourhood_attention(qkv, qkvc, q_g, k_g, rpb):
    b, l, d3 = qkv.shape
    d = d3 // 3
    nctx = qkvc.shape[1]
    hd = d // NA_HEADS
    rows = l // GRID_W
    blocks, patterns = _na_plan(rows)
    nq, nk = NA_QROWS * GRID_W, NA_KROWS * GRID_W
    seq = lambda col0: pl.BlockSpec((1, l, hd), lambda h, bi: (bi, 0, col0 + h))
    ctx = lambda col0: pl.BlockSpec((1, nctx, hd), lambda h, bi: (bi, 0, col0 + h))
    gain = pl.BlockSpec((1, hd), lambda h, bi: (0, 0))
    return pl.pallas_call(
        functools.partial(_na_kernel, blocks=blocks, patterns=patterns, hd=hd),
        grid=(NA_HEADS, b),
        in_specs=[pl.BlockSpec(memory_space=pltpu.SMEM),
                  seq(0), seq(NA_HEADS), seq(2 * NA_HEADS), ctx(NA_HEADS), ctx(2 * NA_HEADS), gain, gain],
        out_specs=pl.BlockSpec((1, l, hd), lambda h, bi: (bi, 0, h)),
        out_shape=jax.ShapeDtypeStruct((b, l, d), BF16),
        scratch_shapes=[pltpu.VMEM((2 * WIN_H - 1, GRID_W, GRID_W), F32),
                        pltpu.VMEM((len(patterns), nq, nk), F32),
                        pltpu.VMEM((l, hd), BF16), pltpu.VMEM((l, hd), BF16), pltpu.VMEM((nctx, hd), BF16)],
        compiler_params=_params("arbitrary", "arbitrary"),
        name="neighbourhood_attention",
    )(rpb.reshape(-1), qkv, qkv, qkv, qkvc, qkvc, q_g.reshape(1, hd), k_g.reshape(1, hd))


def kernel(x, c, ctx, c_ctx, mod_w, mod_b, norm1_g, norm2_g, a_w_pw1, a_b_pw1, a_w_dw, a_b_dw, a_ln_g, a_ln_b,
           a_w_pw2, a_b_pw2, b_w_out, b_b_out, c_w_qkv, c_q_g, c_k_g, c_rpb, c_w_o, f_w_up, f_w_dw, f_b_dw, f_w_down):
    bsz, seq, d = x.shape
    nctx = ctx.shape[1]
    depth = mod_w.shape[0]
    f = f_w_down.shape[1]
    n_mix = 3

    cond = jnp.zeros((MOD_ROWS, d), F32).at[:bsz].set(c).at[bsz].set(c_ctx)
    mods = _modulation(cond, mod_w, mod_b)

    def mod_vecs(i, first_row, n_rows):
        m = mods[i, first_row:first_row + n_rows]
        return [m[:, None, k * d:(k + 1) * d] for k in range(6)]

    ctx_out = [any(l % n_mix == 2 for l in range(i + 1, depth)) for i in range(depth)]
    ctx_in = [ctx_out[i] or i % n_mix == 2 for i in range(depth)]

    flat = lambda t: t.reshape(1, bsz * nctx, t.shape[-1])
    unflat = lambda t: t.reshape(bsz, nctx, t.shape[-1])

    wup_all = _prep_up(f_w_up)
    wdn_all = f_w_down.astype(BF16)

    mx, mc = mod_vecs(0, 0, bsz), mod_vecs(0, bsz, 1)
    h = _normmod(x, norm1_g[0], mx[0], mx[1])
    hc = _normmod(ctx, norm1_g[0], mc[0], mc[1]) if ctx_in[0] else None

    for i in range(depth):
        kind, j = i % n_mix, i // n_mix
        mx, mc = mod_vecs(i, 0, bsz), mod_vecs(i, bsz, 1)
        last = i + 1 == depth
        nmx, nmc = (None, None) if last else (mod_vecs(i + 1, 0, bsz), mod_vecs(i + 1, bsz, 1))

        if kind == 0:
            w1, w2 = a_w_pw1[j].astype(BF16), a_w_pw2[j].astype(BF16)

            def mixer(hh, res, gate, nxt, shared):
                fl, un = (flat, unflat) if shared else ((lambda t: t), (lambda t: t))
                u = un(_matmul_glu(fl(hh), w1, a_b_pw1[j]))
                u = _dwconv(u, a_w_dw[j], a_b_dw[j])
                xo, ho = _gated_matmul(fl(u), w2, a_b_pw2[j], fl(res), gate, ln=(a_ln_g[j], a_ln_b[j]), nxt=nxt)
                return un(xo), un(ho)
        elif kind == 1:
            wo = b_w_out[j].astype(BF16)

            def mixer(hh, res, gate, nxt, shared):
                fl, un = (flat, unflat) if shared else ((lambda t: t), (lambda t: t))
                xo, ho = _gated_matmul(fl(_fourier_mix(hh)), wo, b_b_out[j], fl(res), gate, nxt=nxt)
                return un(xo), un(ho)
        else:
            wqkv, wo = c_w_qkv[j].astype(BF16), c_w_o[j].astype(BF16)
            qkvc = unflat(_matmul(flat(hc), wqkv))

            def mixer(hh, res, gate, nxt, shared):
                assert not shared, "context outputs of the attention mixer are never consumed"
                o = _neighbourhood_attention(_matmul(hh, wqkv), qkvc, c_q_g[j], c_k_g[j], c_rpb[j])
                return _gated_matmul(o, wo, jnp.zeros((d,), F32), res, gate, nxt=nxt)

        wdw = _chunk_pairs(f_w_dw[i], f, FFN_SUB)
        bdw = _chunk_pairs(f_b_dw[i], f, FFN_SUB).reshape(1, 2 * f)

        x, h2 = mixer(h, x, mx[2], (norm2_g[i], mx[3], mx[4]), False)
        x, h = _conv_ffn(h2, x, i, wup_all, wdw, bdw, wdn_all, mx[5],
                         nxt=None if last else (norm1_g[i + 1], nmx[0], nmx[1]))
        if ctx_out[i]:
            ctx, hc2 = mixer(hc, ctx, mc[2], (norm2_g[i], mc[3], mc[4]), True)
            ctx, hc = _conv_ffn(hc2, ctx, i, wup_all, wdw, bdw, wdn_all, mc[5],
                                nxt=(norm1_g[i + 1], nmc[0], nmc[1]) if ctx_in[i + 1] else None)
    return x
```
